```python
import numpy as np
import jax
import jax.numpy as jnp
from jax import lax

D_MODEL = 1024
BATCH = 8
SEQ = 2048
DEPTH = 4
DEC_BATCH = 32
DEC_SEQ = 8
PAST_LEN = 8192
PAGE_SIZE = 128

HEAD_DIM = 64
ATTN_W = D_MODEL // 2
CONV_W = D_MODEL // 4
POOL_W = D_MODEL - ATTN_W - CONV_W
MIX_W = ATTN_W + CONV_W + POOL_W
N_Q_HEADS = ATTN_W // HEAD_DIM
N_KV_HEADS = 2
Q_PER_KV = N_Q_HEADS // N_KV_HEADS
KV_W = N_KV_HEADS * HEAD_DIM
N_BRANCH = 3
CONV_HEADS = CONV_W // HEAD_DIM
CONV_K = 3
CONV_BUF = CONV_K - 1
POOL_WINDOWS = (2, 4, 8, 16)
POOL_GROUPS = len(POOL_WINDOWS)
POOL_GW = POOL_W // POOL_GROUPS
POOL_BUF = max(POOL_WINDOWS) - 1
CMP_BLOCK = 32
CMP_STRIDE = 16
SLC_BLOCK = 64
N_SLC = 16
WINDOW = 512
SLC_QBLOCK = 64
WIN_QBLOCK = 128
FFN_HIDDEN = -(-(8 * D_MODEL) // (3 * 256)) * 256
IN_WIDTHS = (ATTN_W,) + (KV_W,) * 6 + (N_Q_HEADS * N_BRANCH,) + (CONV_W,) * 3 + (POOL_W,)
IN_W = sum(IN_WIDTHS)
IN_SPLITS = tuple(np.cumsum(IN_WIDTHS)[:-1].tolist())
EPS = 1e-6
NEG = -1e30
FORCE = 1e4
ATTN_SCALE = HEAD_DIM ** -0.5

kernel_name = 'hymba_nsa_conv_pool_decoder_step'


def rms_norm(x, g):
    xf = x.astype(jnp.float32)
    y = xf * lax.rsqrt(jnp.mean(xf * xf, axis=-1, keepdims=True) + EPS)
    return (y * g.astype(jnp.float32)).astype(x.dtype)


def masked_softmax(s, mask):
    p = jax.nn.softmax(jnp.where(mask, s, NEG), axis=-1)
    return jnp.where(mask, p, 0.0)


def compress(raw, pe, w1, w2):
    L = raw.shape[1]
    n_cmp = (L - CMP_BLOCK) // CMP_STRIDE + 1
    idx = np.arange(n_cmp)[:, None] * CMP_STRIDE + np.arange(CMP_BLOCK)[None, :]
    blk = raw[:, idx] + pe[:, None, :]
    blk = jnp.moveaxis(blk, 3, 2)
    flat = blk.reshape(blk.shape[:3] + (CMP_BLOCK * HEAD_DIM,))
    return jax.nn.silu(flat @ w1) @ w2


def cmp_attention(q, k_cmp, v_cmp, q_pos):
    n_cmp = k_cmp.shape[1]
    blk_end = np.arange(n_cmp) * CMP_STRIDE + CMP_BLOCK - 1
    mask = blk_end[None, :] <= q_pos[:, None]
    s = jnp.einsum('btgrd,bcgd->bgrtc', q, k_cmp).astype(jnp.float32) * ATTN_SCALE
    p = masked_softmax(s, mask)
    o = jnp.einsum('bgrtc,bcgd->btgrd', p.astype(v_cmp.dtype), v_cmp)
    return o, p


def cmp_to_slc_matrix(n_cmp, n_sel):
    m = np.zeros((n_cmp, n_sel), np.float32)
    i = np.arange(n_cmp)
    for part in range(CMP_BLOCK // CMP_STRIDE):
        j = np.minimum((i + part) * CMP_STRIDE // SLC_BLOCK, n_sel - 1)
        np.add.at(m, (i, j), 1.0)
    return m


def select_blocks(p_cmp, q_pos, n_sel):
    overlap = jnp.asarray(cmp_to_slc_matrix(p_cmp.shape[-1], n_sel))
    imp = jnp.einsum('bgrtc,cj->bgtj', p_cmp, overlap)
    j = np.arange(n_sel)[None, :]
    cur = (q_pos // SLC_BLOCK)[:, None]
    valid = j <= cur
    forced = valid & ((j == 0) | (j == cur) | (j == cur - 1))
    score = jnp.where(forced, FORCE, jnp.where(valid, imp, NEG))
    top_s, top_i = lax.top_k(score, min(N_SLC, n_sel))
    return top_i, top_s > 0.5 * NEG


def slc_block_attention(q_b, idx_b, ok_b, pos_b, k_blocks, v_blocks):
    b_i = jnp.arange(k_blocks.shape[0])[:, None, None, None]
    g_i = jnp.arange(k_blocks.shape[1])[None, :, None, None]
    k_sel = k_blocks[b_i, g_i, idx_b]
    v_sel = v_blocks[b_i, g_i, idx_b]
    s = jnp.einsum('btgrd,bgtkpd->bgrtkp', q_b, k_sel).astype(jnp.float32) * ATTN_SCALE
    k_pos = idx_b[..., None] * SLC_BLOCK + jnp.arange(SLC_BLOCK)
    mask = (ok_b[..., None] & (k_pos <= pos_b[None, None, :, None, None]))[:, :, None]
    shp = s.shape
    p = masked_softmax(s.reshape(shp[:4] + (-1,)), mask.reshape(mask.shape[:4] + (-1,))).reshape(shp)
    return jnp.einsum('bgrtkp,bgtkpd->btgrd', p.astype(v_sel.dtype), v_sel)


def slc_attention(q, top_i, ok, q_pos, k_slc, v_slc):
    B, T = q.shape[:2]
    L = k_slc.shape[1]
    n_sel = -(-L // SLC_BLOCK)

    def to_blocks(a):
        a = jnp.pad(a, ((0, 0), (0, n_sel * SLC_BLOCK - L), (0, 0), (0, 0)))
        return a.reshape(B, n_sel, SLC_BLOCK, N_KV_HEADS, HEAD_DIM).transpose(0, 3, 1, 2, 4)

    k_blocks, v_blocks = to_blocks(k_slc), to_blocks(v_slc)
    qb = SLC_QBLOCK if T % SLC_QBLOCK == 0 else T
    nb = T // qb
    kk = top_i.shape[-1]
    xs = (jnp.moveaxis(q.reshape((B, nb, qb) + q.shape[2:]), 1, 0),
          jnp.moveaxis(top_i.reshape(B, N_KV_HEADS, nb, qb, kk), 2, 0),
          jnp.moveaxis(ok.reshape(B, N_KV_HEADS, nb, qb, kk), 2, 0),
          jnp.asarray(q_pos.reshape(nb, qb), jnp.int32))
    out = lax.map(lambda a: slc_block_attention(a[0], a[1], a[2], a[3], k_blocks, v_blocks), xs)
    return jnp.moveaxis(out, 0, 1).reshape(q.shape)


def window_attention(q, k, v, q_pos, k_pos):
    s = jnp.einsum('bnqgrd,bnkgd->bngrqk', q, k).astype(jnp.float32) * ATTN_SCALE
    diff = q_pos[:, :, None] - k_pos[:, None, :]
    mask = (k_pos[:, None, :] >= 0) & (diff >= 0) & (diff < WINDOW)
    p = masked_softmax(s, mask[None, :, None, None])
    return jnp.einsum('bngrqk,bnkgd->bnqgrd', p.astype(v.dtype), v)


def window_prompt(q, kw, vw):
    B, T = q.shape[:2]
    nb = T // WIN_QBLOCK
    n_prev = WINDOW // WIN_QBLOCK

    def band(a):
        a = jnp.pad(a, ((0, 0), (WINDOW, 0), (0, 0), (0, 0)))
        a = a.reshape(B, nb + n_prev, WIN_QBLOCK, N_KV_HEADS, HEAD_DIM)
        return jnp.concatenate([a[:, j:j + nb] for j in range(n_prev + 1)], axis=2)

    q_pos = np.arange(T).reshape(nb, WIN_QBLOCK)
    k_pos = np.arange(nb)[:, None] * WIN_QBLOCK - WINDOW + np.arange((n_prev + 1) * WIN_QBLOCK)[None, :]
    qb = q.reshape((B, nb, WIN_QBLOCK) + q.shape[2:])
    return window_attention(qb, band(kw), band(vw), q_pos, k_pos).reshape(q.shape)


def window_sample(q, kw_ext, vw_ext, pos0):
    T = q.shape[1]
    n_k = kw_ext.shape[1]
    q_pos = (pos0 + np.arange(T))[None]
    k_pos = (pos0 + T - n_k + np.arange(n_k))[None]
    return window_attention(q[:, None], kw_ext[:, None], vw_ext[:, None], q_pos, k_pos)[:, 0]


def short_conv(u_ext, w, bias):
    T = u_ext.shape[1] - CONV_BUF
    out = bias
    for j in range(CONV_K):
        out = out + u_ext[:, j:j + T] * w[j]
    return out


def multi_pool(p_ext, pos0, pool_w, pool_scale):
    B, n_ext, _ = p_ext.shape
    T = n_ext - POOL_BUF
    cs = jnp.cumsum(jnp.pad(p_ext.astype(jnp.float32), ((0, 0), (1, 0), (0, 0))), axis=1)
    pos = pos0 + np.arange(T)
    hi = cs[:, POOL_BUF + 1:]
    means = []
    for g, w in enumerate(POOL_WINDOWS):
        ch = slice(g * POOL_GW, (g + 1) * POOL_GW)
        lo = cs[:, POOL_BUF + 1 - w: POOL_BUF + 1 - w + T, ch]
        cnt = np.minimum(w, pos + 1).astype(np.float32)[None, :, None]
        means.append((hi[:, :, ch] - lo) / cnt)
    d = jnp.concatenate(means, axis=-1) - p_ext[:, POOL_BUF:].astype(jnp.float32)
    d = d.astype(p_ext.dtype).reshape(B, T, POOL_GROUPS, POOL_GW)
    return jnp.einsum('btgc,gcd->btgd', d, pool_w).reshape(B, T, POOL_W) * pool_scale


def swiglu(h, w_up, w_down):
    a, b = jnp.split(h @ w_up, 2, axis=-1)
    return (jax.nn.silu(a) * b) @ w_down


def trunk_layer(x, c, kv_past, win_past, conv_past, pool_past, lw):
    (norm1, norm2, w_ada, b_ada, w_in, w_out, q_gain, k_gain, cmp_pe, cmp_w1, cmp_w2,
     conv_w, conv_bias, pool_w, pool_scale, w_up, w_down) = lw
    B, T, _ = x.shape
    pos0 = 0 if kv_past is None else kv_past.shape[1]
    q_pos = pos0 + np.arange(T)
    mod = (jax.nn.silu(c) @ w_ada + b_ada)[:, None, :]
    shift1, scale1, gate1, shift2, scale2, gate2 = jnp.split(mod, 6, axis=-1)

    h = rms_norm(x, norm1) * (1 + scale1) + shift1
    z = h @ w_in
    (q, kc, vc, ks, vs, kw, vw, gates, conv_bg, conv_cg, conv_h, pool_in) = jnp.split(z, IN_SPLITS, axis=-1)

    heads = lambda a: a.reshape(B, T, N_KV_HEADS, HEAD_DIM)
    q = rms_norm(q.reshape(B, T, N_KV_HEADS, Q_PER_KV, HEAD_DIM), q_gain)
    ks = rms_norm(heads(ks), k_gain[1])
    kw = rms_norm(heads(kw), k_gain[2])
    new_kv = jnp.stack([heads(kc), heads(vc), ks, heads(vs)], axis=2)
    new_win = jnp.stack([kw, heads(vw)], axis=2)
    full = new_kv if kv_past is None else jnp.concatenate([kv_past, new_kv], axis=1)
    k_cmp = rms_norm(compress(full[:, :, 0], cmp_pe[0], cmp_w1[0], cmp_w2[0]), k_gain[0])
    v_cmp = compress(full[:, :, 1], cmp_pe[1], cmp_w1[1], cmp_w2[1])
    o_cmp, p_cmp = cmp_attention(q, k_cmp, v_cmp, q_pos)
    top_i, sel_ok = select_blocks(p_cmp, q_pos, -(-full.shape[1] // SLC_BLOCK))
    o_slc = slc_attention(q, top_i, sel_ok, q_pos, full[:, :, 2], full[:, :, 3])
    if win_past is None:
        o_win = window_prompt(q, new_win[:, :, 0], new_win[:, :, 1])
        win_state = new_win[:, -min(WINDOW, T):]
    else:
        win_ext = jnp.concatenate([win_past, new_win], axis=1)
        o_win = window_sample(q, win_ext[:, :, 0], win_ext[:, :, 1], pos0)
        win_state = win_ext[:, -win_past.shape[1]:]
    g = jax.nn.sigmoid(gates.astype(jnp.float32)).astype(x.dtype).reshape(B, T, N_KV_HEADS, Q_PER_KV, N_BRANCH)
    o_attn = (g[..., 0:1] * o_cmp + g[..., 1:2] * o_slc + g[..., 2:3] * o_win).reshape(B, T, ATTN_W)

    u_ext = jnp.concatenate([conv_past, conv_cg * conv_h], axis=1)
    y_conv = conv_bg * short_conv(u_ext, conv_w, conv_bias)

    p_ext = jnp.concatenate([pool_past, pool_in], axis=1)
    y_pool = multi_pool(p_ext, pos0, pool_w, pool_scale)

    x = x + gate1 * (jnp.concatenate([o_attn, y_conv, y_pool], axis=-1) @ w_out)
    h = rms_norm(x, norm2) * (1 + scale2) + shift2
    x = x + gate2 * swiglu(h, w_up, w_down)
    return x, new_kv, win_state, u_ext[:, -CONV_BUF:], p_ext[:, -POOL_BUF:]


def setup_inputs(seed: int = 0) -> dict:
    key = jax.random.key(seed)
    k = jax.random.split(key, 26)
    n_pages = PAST_LEN // PAGE_SIZE
    n_pool = (DEC_BATCH * n_pages * 5) // 4
    win_buf = min(WINDOW, PAST_LEN)
    D = D_MODEL

    def nrm(kk, shape, scale=1.0):
        return jax.random.normal(kk, shape, jnp.float32) * scale

    def gain(kk, shape):
        return 1.0 + 0.1 * jax.random.normal(kk, shape, jnp.float32)

    page_table = jax.random.permutation(k[6], n_pool)[: DEC_BATCH * n_pages].reshape(DEC_BATCH, n_pages).astype(jnp.int32)
    return {
        'x_prompt': nrm(k[0], (BATCH, SEQ, D)),
        'x_sample': nrm(k[1], (DEC_BATCH, DEC_SEQ, D)),
        'cache_nsa_kv': nrm(k[2], (DEPTH, n_pool, PAGE_SIZE, 4, N_KV_HEADS, HEAD_DIM)),
        'state_win_kv': nrm(k[3], (DEPTH, DEC_BATCH, win_buf, 2, N_KV_HEADS, HEAD_DIM)),
        'state_conv': nrm(k[4], (DEPTH, DEC_BATCH, CONV_BUF, CONV_W)),
        'state_pool': nrm(k[5], (DEPTH, DEC_BATCH, POOL_BUF, POOL_W)),
        'page_table': page_table,
        'c_prompt': nrm(k[7], (BATCH, D)),
        'c_sample': nrm(k[8], (DEC_BATCH, D)),
        'norm_mix': gain(k[9], (DEPTH, D)),
        'norm_ffn': gain(k[10], (DEPTH, D)),
        'w_ada': nrm(k[11], (DEPTH, D, 6 * D), 0.5 * D ** -0.5),
        'b_ada': nrm(k[12], (DEPTH, 6 * D), 0.01),
        'w_in': nrm(k[13], (DEPTH, D, IN_W), D ** -0.5),
        'w_out': nrm(k[14], (DEPTH, MIX_W, D), MIX_W ** -0.5),
        'q_norm': gain(k[15], (DEPTH, HEAD_DIM)),
        'k_norm': gain(k[16], (DEPTH, N_BRANCH, HEAD_DIM)),
        'cmp_pe': nrm(k[17], (DEPTH, 2, CMP_BLOCK, HEAD_DIM), 0.1),
        'cmp_w1': nrm(k[18], (DEPTH, 2, CMP_BLOCK * HEAD_DIM, HEAD_DIM), (CMP_BLOCK * HEAD_DIM) ** -0.5),
        'cmp_w2': nrm(k[19], (DEPTH, 2, HEAD_DIM, HEAD_DIM), HEAD_DIM ** -0.5),
        'conv_w': nrm(k[20], (DEPTH, CONV_K, CONV_W), CONV_K ** -0.5),
        'conv_bias': nrm(k[21], (DEPTH, CONV_W), 0.01),
        'pool_w': nrm(k[22], (DEPTH, POOL_GROUPS, POOL_GW, POOL_GW), POOL_GW ** -0.5),
        'pool_scale': gain(k[23], (DEPTH, POOL_W)),
        'w_up': nrm(k[24], (DEPTH, D, 2 * FFN_HIDDEN), D ** -0.5),
        'w_down': nrm(k[25], (DEPTH, FFN_HIDDEN, D), FFN_HIDDEN ** -0.5),
    }


def reference(x_prompt, x_sample, cache_nsa_kv, state_win_kv, state_conv, state_pool, page_table,
              c_prompt, c_sample, norm_mix, norm_ffn, w_ada, b_ada, w_in, w_out, q_norm, k_norm,
              cmp_pe, cmp_w1, cmp_w2, conv_w, conv_bias, pool_w, pool_scale, w_up, w_down):
    n_dec, n_pages = page_table.shape
    page = cache_nsa_kv.shape[2]
    yp, ys = x_prompt, x_sample
    conv0 = jnp.zeros((x_prompt.shape[0], CONV_BUF, CONV_W), x_prompt.dtype)
    pool0 = jnp.zeros((x_prompt.shape[0], POOL_BUF, POOL_W), x_prompt.dtype)
    kv_p, kv_s, win_p, win_s, conv_p, conv_s, pool_p, pool_s = [], [], [], [], [], [], [], []
    for l in range(DEPTH):
        lw = (norm_mix[l], norm_ffn[l], w_ada[l], b_ada[l], w_in[l], w_out[l], q_norm[l], k_norm[l],
              cmp_pe[l], cmp_w1[l], cmp_w2[l], conv_w[l], conv_bias[l], pool_w[l], pool_scale[l],
              w_up[l], w_down[l])
        yp, a, b, cst, pst = trunk_layer(yp, c_prompt, None, None, conv0, pool0, lw)
        kv_p.append(a); win_p.append(b); conv_p.append(cst); pool_p.append(pst)
        kv_past = cache_nsa_kv[l, page_table].reshape(n_dec, n_pages * page, 4, N_KV_HEADS, HEAD_DIM)
        ys, a, b, cst, pst = trunk_layer(ys, c_sample, kv_past, state_win_kv[l], state_conv[l], state_pool[l], lw)
        kv_s.append(a); win_s.append(b); conv_s.append(cst); pool_s.append(pst)
    return (yp, ys, jnp.stack(kv_p), jnp.stack(kv_s), jnp.stack(win_p), jnp.stack(win_s),
            jnp.stack(conv_p), jnp.stack(conv_s), jnp.stack(pool_p), jnp.stack(pool_s))
```

```python
import functools

import numpy as np
import jax
import jax.numpy as jnp
from jax import lax
from jax.experimental import pallas as pl
from jax.experimental.pallas import tpu as pltpu

F32 = jnp.float32
BF16 = jnp.bfloat16

HEAD_DIM = 64
N_Q_HEADS = 8
N_KV_HEADS = 2
Q_PER_KV = N_Q_HEADS // N_KV_HEADS
N_BRANCH = 3
ATTN_W = N_Q_HEADS * HEAD_DIM
KV_W = N_KV_HEADS * HEAD_DIM
CONV_W = 256
POOL_W = 256
CONV_K = 3
CONV_BUF = CONV_K - 1
POOL_WINDOWS = (2, 4, 8, 16)
POOL_BUF = max(POOL_WINDOWS) - 1
CMP_BLOCK = 32
CMP_STRIDE = 16
SLC_BLOCK = 64
N_SLC = 16
WINDOW = 512
EPS = 1e-6
NEG = -1e30
FORCE = 1e4
ATTN_SCALE = HEAD_DIM ** -0.5
GATE_PAD = 128
IN_W_PAD = ATTN_W + 4 * KV_W + 2 * KV_W + 3 * CONV_W + POOL_W + GATE_PAD
CONV_HALO = 8
POOL_HALO = 16
NEW_PAD = 128
VMEM_LIMIT = 56 * 1024 * 1024

_NT = (((1,), (1,)), ((), ()))


def _params(*sem):
    return pltpu.CompilerParams(dimension_semantics=sem, vmem_limit_bytes=VMEM_LIMIT)


def _dot(a, b):
    return jnp.dot(a, b, preferred_element_type=F32)


def _dot_nt(a, b):
    return lax.dot_general(a, b, _NT, preferred_element_type=F32)


def _dot_hilo(x, w_bf):
    hi = x.astype(BF16)
    lo = (x - hi.astype(F32)).astype(BF16)
    return _dot(hi, w_bf) + _dot(lo, w_bf)


def _seg_rms(x, seg_bf, gain):
    ms = _dot_hilo(x * x, seg_bf)
    return x * lax.rsqrt(ms + EPS) * gain


def _silu(x):
    return x * jax.nn.sigmoid(x)


def _pad_rows(x, n):
    return jnp.concatenate([x, jnp.zeros((n - x.shape[0], x.shape[1]), x.dtype)], axis=0)


def _ada_kernel(c_ref, w_ref, b_ref, o_ref):
    c = c_ref[...]
    o_ref[0] = _dot(_silu(c).astype(BF16), w_ref[0].astype(BF16)) + b_ref[0]


def _ada(c_all, w_ada, b_ada):
    depth, d, n = w_ada.shape
    r = c_all.shape[0]
    tn = 1536
    return pl.pallas_call(
        _ada_kernel,
        grid=(depth, n // tn),
        in_specs=[pl.BlockSpec((r, d), lambda l, j: (0, 0)),
                  pl.BlockSpec((1, d, tn), lambda l, j: (l, 0, j)),
                  pl.BlockSpec((1, 1, tn), lambda l, j: (l, 0, j))],
        out_specs=pl.BlockSpec((1, r, tn), lambda l, j: (l, 0, j)),
        out_shape=jax.ShapeDtypeStruct((depth, r, n), F32),
        compiler_params=_params("parallel", "parallel"),
        name="ada",
    )(c_all, w_ada, b_ada.reshape(depth, 1, n))


def _proj_in_kernel(x_ref, mod_ref, norm_ref, w_ref, qg_ref, kg_ref, seg_q_ref, seg_k_ref,
                    q_ref, kv_ref, win_ref, bg_ref, u_ref, pool_ref, gates_ref, *bf_refs):
    bb, tt, d = x_ref.shape
    rows = bb * tt
    x = x_ref[...]
    y = x * lax.rsqrt(jnp.mean(x * x, axis=-1, keepdims=True) + EPS) * norm_ref[...]
    shift1 = mod_ref[:, :, 0:d]
    scale1 = mod_ref[:, :, d:2 * d]
    h = (y * (1.0 + scale1) + shift1).reshape(rows, d).astype(BF16)
    z = _dot(h, w_ref[...])

    qn = _seg_rms(z[:, 0:ATTN_W], seg_q_ref[...], qg_ref[...]) * ATTN_SCALE
    for hh in range(N_Q_HEADS):
        q_ref[:, hh] = qn[:, hh * HEAD_DIM:(hh + 1) * HEAD_DIM].reshape(bb, tt, HEAD_DIM)

    o = ATTN_W
    raw_cmp = z[:, o:o + 2 * KV_W]
    ks = _seg_rms(z[:, o + 2 * KV_W:o + 3 * KV_W], seg_k_ref[...], kg_ref[0:1])
    vs = z[:, o + 3 * KV_W:o + 4 * KV_W]
    kv_ref[...] = jnp.concatenate([raw_cmp, ks, vs], axis=-1).reshape(bb, tt, 4 * KV_W)
    o += 4 * KV_W
    kw = _seg_rms(z[:, o:o + KV_W], seg_k_ref[...], kg_ref[1:2])
    vw = z[:, o + KV_W:o + 2 * KV_W]
    win = jnp.concatenate([kw, vw], axis=-1)
    win_ref[...] = win.reshape(bb, tt, 2 * KV_W)
    if bf_refs:
        bf_refs[0][...] = jnp.concatenate([ks, vs], axis=-1).reshape(bb, tt, 2 * KV_W).astype(BF16)
        bf_refs[1][...] = win.reshape(bb, tt, 2 * KV_W).astype(BF16)
    o += 2 * KV_W
    bg_ref[...] = z[:, o:o + CONV_W].reshape(bb, tt, CONV_W)
    u_ref[...] = (z[:, o + CONV_W:o + 2 * CONV_W] * z[:, o + 2 * CONV_W:o + 3 * CONV_W]).reshape(bb, tt, CONV_W)
    o += 3 * CONV_W
    pool_ref[...] = z[:, o:o + POOL_W].reshape(bb, tt, POOL_W)
    o += POOL_W
    gates_ref[...] = jax.nn.sigmoid(z[:, o:o + GATE_PAD]).reshape(bb, tt, GATE_PAD)


def _proj_in(x, mod, norm1, w_in, q_gain, k_gain2, seg_q, seg_k, bb, tt, emit_bf16):
    bx, t, d = x.shape
    grid = (bx // bb, t // tt)
    blk = lambda w: pl.BlockSpec((bb, tt, w), lambda b, i: (b, i, 0))
    full = lambda a: pl.BlockSpec(a.shape, lambda b, i: (0,) * a.ndim)
    outs = [((bx, N_Q_HEADS, t, HEAD_DIM), F32), ((bx, t, 4 * KV_W), F32), ((bx, t, 2 * KV_W), F32),
            ((bx, t, CONV_W), F32), ((bx, t, CONV_W), F32), ((bx, t, POOL_W), F32), ((bx, t, GATE_PAD), F32)]
    if emit_bf16:
        outs += [((bx, t, 2 * KV_W), BF16), ((bx, t, 2 * KV_W), BF16)]
    out_specs = [pl.BlockSpec((bb, N_Q_HEADS, tt, HEAD_DIM), lambda b, i: (b, 0, i, 0))]
    out_specs += [blk(s[-1]) for s, _ in outs[1:]]
    return pl.pallas_call(
        _proj_in_kernel,
        grid=grid,
        in_specs=[blk(d), pl.BlockSpec((bb, 1, mod.shape[-1]), lambda b, i: (b, 0, 0)),
                  full(norm1), full(w_in), full(q_gain), full(k_gain2), full(seg_q), full(seg_k)],
        out_specs=out_specs,
        out_shape=[jax.ShapeDtypeStruct(s, dt) for s, dt in outs],
        compiler_params=_params("parallel", "parallel"),
        name="proj_in",
    )(x, mod, norm1, w_in, q_gain, k_gain2, seg_q, seg_k)


def _compress_core(read_rows, nc, wr_ref, pe_ref, w2_ref, seg_k_ref, kg0_ref):
    outs = []
    for s in range(2):
        acc = jnp.zeros((nc, 2 * KV_W), F32)
        acc_pe = jnp.zeros((8, 2 * KV_W), F32)
        for r in range(CMP_STRIDE):
            w = wr_ref[r, s]
            acc = acc + _dot(read_rows(s, r).astype(BF16), w)
            acc_pe = acc_pe + _dot(pe_ref[r, s].astype(BF16), w)
        bias = acc_pe[0:1, 0:KV_W] + acc_pe[1:2, KV_W:2 * KV_W]
        nxt = pltpu.roll(acc[:, KV_W:2 * KV_W], nc - 1, 0)
        pre = acc[:, 0:KV_W] + nxt + bias
        outs.append(_dot(_silu(pre).astype(BF16), w2_ref[s]))
    return _seg_rms(outs[0], seg_k_ref[...], kg0_ref[...]), outs[1]


def _cmp_prompt_kernel(k_ref, v_ref, wr_ref, pe_ref, w2_ref, seg_k_ref, kg0_ref, kc_ref, vc_ref):
    nc = kc_ref.shape[1]
    src = (k_ref, v_ref)
    read = lambda s, r: src[s][0, pl.ds(r, nc, stride=CMP_STRIDE), :]
    kc, vc = _compress_core(read, nc, wr_ref, pe_ref, w2_ref, seg_k_ref, kg0_ref)
    kc_ref[0] = kc.astype(BF16)
    vc_ref[0] = vc.astype(BF16)


def _cmp_prompt(kv, wr, pe_aug, w2bd, seg_k, kg0):
    bx, t, _ = kv.shape
    nc = t // CMP_STRIDE
    full = lambda a: pl.BlockSpec(a.shape, lambda b: (0,) * a.ndim)
    return pl.pallas_call(
        _cmp_prompt_kernel,
        grid=(bx,),
        in_specs=[pl.BlockSpec((1, t, KV_W), lambda b: (b, 0, 0)),
                  pl.BlockSpec((1, t, KV_W), lambda b: (b, 0, 1)),
                  full(wr), full(pe_aug), full(w2bd), full(seg_k), full(kg0)],
        out_specs=[pl.BlockSpec((1, nc, KV_W), lambda b: (b, 0, 0))] * 2,
        out_shape=[jax.ShapeDtypeStruct((bx, nc, KV_W), BF16)] * 2,
        compiler_params=_params("parallel"),
        name="cmp_prompt",
    )(kv, kv, wr, pe_aug, w2bd, seg_k, kg0)


def _cmp_sample_kernel(pt_ref, *refs, pps):
    pages = refs[:pps]
    wr_ref, pe_ref, w2_ref, seg_k_ref, kg0_ref, kc_ref, vc_ref, xk_ref, xv_ref = refs[pps:]
    step = pl.program_id(1)
    page = pages[0].shape[2]
    for k in range(pps):
        rows = pl.ds(pl.multiple_of((step * pps + k) * page, page), page)
        xk_ref[rows, :] = pages[k][0, 0, :, 0:KV_W]
        xv_ref[rows, :] = pages[k][0, 0, :, KV_W:2 * KV_W]

    @pl.when(step == pl.num_programs(1) - 1)
    def _():
        nc = kc_ref.shape[1]
        src = (xk_ref, xv_ref)
        read = lambda s, r: src[s][pl.ds(r, nc, stride=CMP_STRIDE), :]
        kc, vc = _compress_core(read, nc, wr_ref, pe_ref, w2_ref, seg_k_ref, kg0_ref)
        kc_ref[0] = kc.astype(BF16)
        vc_ref[0] = vc.astype(BF16)


def _cmp_sample(cache4, layer, page_table, wr, pe_aug, w2bd, seg_k, kg0, pps):
    bx, n_pages = page_table.shape
    page = cache4.shape[2]
    past = n_pages * page
    nc = past // CMP_STRIDE
    full = lambda a: pl.BlockSpec(a.shape, lambda b, s, pt: (0,) * a.ndim)
    page_specs = [pl.BlockSpec((1, 1, page, 2 * KV_W),
                               lambda b, s, pt, k=k: (layer, pt[b, s * pps + k], 0, 0))
                  for k in range(pps)]
    return pl.pallas_call(
        functools.partial(_cmp_sample_kernel, pps=pps),
        grid_spec=pltpu.PrefetchScalarGridSpec(
            num_scalar_prefetch=1,
            grid=(bx, n_pages // pps),
            in_specs=page_specs + [full(wr), full(pe_aug), full(w2bd), full(seg_k), full(kg0)],
            out_specs=[pl.BlockSpec((1, nc, KV_W), lambda b, s, pt: (b, 0, 0))] * 2,
            scratch_shapes=[pltpu.VMEM((past, KV_W), F32), pltpu.VMEM((past, KV_W), F32)]),
        out_shape=[jax.ShapeDtypeStruct((bx, nc, KV_W), BF16)] * 2,
        compiler_params=_params("parallel", "arbitrary"),
        name="cmp_sample",
    )(page_table, *([cache4] * pps), wr, pe_aug, w2bd, seg_k, kg0)


def _cmp_branch(qg, kc, vc, cmask, ov_bf):
    nc = kc.shape[0]
    t = qg.shape[0] // Q_PER_KV
    s3 = _dot_nt(qg, kc).reshape(Q_PER_KV, t, nc)
    sm = jnp.where(cmask, s3, NEG)
    m = jnp.max(sm, axis=-1, keepdims=True)
    e = jnp.where(cmask, jnp.exp(sm - m), 0.0)
    l = jnp.sum(e, axis=-1, keepdims=True)
    p = e / jnp.where(l > 0.0, l, 1.0)
    o = _dot(p.reshape(Q_PER_KV * t, nc).astype(BF16), vc)
    psum = p[0] + p[1] + p[2] + p[3]
    return o, _dot_hilo(psum, ov_bf)


def _select(imp, cur, n_sel):
    jidx = lax.broadcasted_iota(jnp.int32, imp.shape, 1)
    valid = jidx <= cur
    forced = valid & ((jidx == 0) | (jidx == cur) | (jidx == cur - 1))
    sc = jnp.where(forced, FORCE, jnp.where(valid, imp, NEG))
    rank = jnp.zeros(imp.shape, jnp.int32)
    for i in range(n_sel):
        col = sc[:, i:i + 1]
        beats = (col > sc) | ((col == sc) & (jidx > i))
        rank = rank + jnp.where(beats, 1, 0)
    return (rank < min(N_SLC, n_sel)) & valid


def _masked_softmax_pv(s3, mask3, v):
    r, t, n = s3.shape
    sm = jnp.where(mask3, s3, NEG)
    m = jnp.max(sm, axis=-1, keepdims=True)
    e = jnp.where(mask3, jnp.exp(sm - m), 0.0)
    l = jnp.sum(e, axis=-1, keepdims=True)
    o = _dot(e.reshape(r * t, n).astype(BF16), v).reshape(r, t, HEAD_DIM)
    return o / jnp.where(l > 0.0, l, 1.0)


def _online_step(carry, s3, mask3, v):
    m, l, acc = carry
    r, t, n = s3.shape
    sm = jnp.where(mask3, s3, NEG)
    m_new = jnp.maximum(m, jnp.max(sm, axis=-1, keepdims=True))
    alpha = jnp.exp(m - m_new)
    e = jnp.where(mask3, jnp.exp(sm - m_new), 0.0)
    l = alpha * l + jnp.sum(e, axis=-1, keepdims=True)
    acc = alpha * acc + _dot(e.reshape(r * t, n).astype(BF16), v).reshape(r, t, HEAD_DIM)
    return m_new, l, acc


def _expand_blocks(sel_bf, k0, n):
    lanes = sel_bf.shape[1]
    jrow = lax.broadcasted_iota(jnp.int32, (lanes, n), 0)
    kcol = lax.broadcasted_iota(jnp.int32, (lanes, n), 1)
    e = jnp.where(jrow == (k0 + kcol) // SLC_BLOCK, 1.0, 0.0).astype(BF16)
    return _dot(sel_bf, e)


def _gate_combine(gates, g, o_cmp, o_slc, o_win):
    outs = []
    for r in range(Q_PER_KV):
        c = (g * Q_PER_KV + r) * N_BRANCH
        outs.append(gates[:, c:c + 1] * o_cmp[r] + gates[:, c + 1:c + 2] * o_slc[r]
                    + gates[:, c + 2:c + 3] * o_win[r])
    return outs


def _attn_prompt_kernel(q_ref, kc_ref, vc_ref, kvs_ref, win_ref, gates_ref, ov_ref, o_ref, *, chunk):
    tq = q_ref.shape[2]
    t_all = kvs_ref.shape[1]
    nc = kc_ref.shape[1]
    n_sel = t_all // SLC_BLOCK
    t0 = pl.program_id(1) * tq
    qpos = t0 + lax.broadcasted_iota(jnp.int32, (tq, 1), 0)
    gates = gates_ref[0]
    cidx = lax.broadcasted_iota(jnp.int32, (1, tq, nc), 2)
    cmask = (cidx * CMP_STRIDE + CMP_BLOCK - 1) <= qpos[None]
    wstart = pl.multiple_of(jnp.maximum(t0 - WINDOW, 0), tq)
    wn = WINDOW + tq
    wdiff = qpos - (wstart + lax.broadcasted_iota(jnp.int32, (tq, wn), 1))
    wmask = ((wdiff >= 0) & (wdiff < WINDOW))[None]
    n_chunks = (t0 + tq + chunk - 1) // chunk
    heads = []
    for g in range(N_KV_HEADS):
        lo, hi = g * HEAD_DIM, (g + 1) * HEAD_DIM
        qg = q_ref[0, g * Q_PER_KV:(g + 1) * Q_PER_KV].reshape(Q_PER_KV * tq, HEAD_DIM).astype(BF16)
        o_cmp, imp = _cmp_branch(qg, kc_ref[0][:, lo:hi], vc_ref[0][:, lo:hi], cmask, ov_ref[...])
        sel = _select(imp, qpos // SLC_BLOCK, n_sel)
        sel_bf = jnp.where(sel, 1.0, 0.0).astype(BF16)

        def body(c, carry):
            k0 = pl.multiple_of(c * chunk, chunk)
            blk = kvs_ref[0, pl.ds(k0, chunk), :]
            s3 = _dot_nt(qg, blk[:, lo:hi]).reshape(Q_PER_KV, tq, chunk)
            kpos = k0 + lax.broadcasted_iota(jnp.int32, (tq, chunk), 1)
            mask = (_expand_blocks(sel_bf, k0, chunk) > 0.5) & (kpos <= qpos)
            return _online_step(carry, s3, mask[None], blk[:, KV_W + lo:KV_W + hi])

        init = (jnp.full((Q_PER_KV, tq, 1), NEG, F32), jnp.zeros((Q_PER_KV, tq, 1), F32),
                jnp.zeros((Q_PER_KV, tq, HEAD_DIM), F32))
        _, l, acc = lax.fori_loop(0, n_chunks, body, init)
        o_slc = acc / l

        wblk = win_ref[0, pl.ds(wstart, wn), :]
        s3 = _dot_nt(qg, wblk[:, lo:hi]).reshape(Q_PER_KV, tq, wn)
        o_win = _masked_softmax_pv(s3, wmask, wblk[:, KV_W + lo:KV_W + hi])
        heads += _gate_combine(gates, g, o_cmp.reshape(Q_PER_KV, tq, HEAD_DIM), o_slc, o_win)
    o_ref[0] = jnp.concatenate(heads, axis=-1).astype(BF16)


def _attn_prompt(q, kc, vc, kvs_bf, win_bf, gates, ov, tq, chunk):
    bx, _, t, _ = q.shape
    nc = kc.shape[1]
    return pl.pallas_call(
        functools.partial(_attn_prompt_kernel, chunk=chunk),
        grid=(bx, t // tq),
        in_specs=[pl.BlockSpec((1, N_Q_HEADS, tq, HEAD_DIM), lambda b, i: (b, 0, i, 0)),
                  pl.BlockSpec((1, nc, KV_W), lambda b, i: (b, 0, 0)),
                  pl.BlockSpec((1, nc, KV_W), lambda b, i: (b, 0, 0)),
                  pl.BlockSpec((1, t, 2 * KV_W), lambda b, i: (b, 0, 0)),
                  pl.BlockSpec((1, t, 2 * KV_W), lambda b, i: (b, 0, 0)),
                  pl.BlockSpec((1, tq, GATE_PAD), lambda b, i: (b, i, 0)),
                  pl.BlockSpec(ov.shape, lambda b, i: (0, 0))],
        out_specs=pl.BlockSpec((1, tq, ATTN_W), lambda b, i: (b, i, 0)),
        out_shape=jax.ShapeDtypeStruct((bx, t, ATTN_W), BF16),
        compiler_params=_params("parallel", "parallel"),
        name="attn_prompt",
    )(q, kc, vc, kvs_bf, win_bf, gates, ov)


def _attn_sample_a_kernel(q_ref, kc_ref, vc_ref, wst_ref, wnew_ref, ov_ref, ocmp_ref, owin_ref, sel_ref,
                          *, pos0, n_cmp, n_sel):
    t = q_ref.shape[2]
    nc = kc_ref.shape[1]
    n_state = wst_ref.shape[1]
    tcol = lax.broadcasted_iota(jnp.int32, (t, 1), 0)
    qpos = pos0 + tcol
    cidx = lax.broadcasted_iota(jnp.int32, (1, t, nc), 2)
    cmask = (cidx < n_cmp) & ((cidx * CMP_STRIDE + CMP_BLOCK - 1) <= qpos[None])
    sdiff = qpos - (pos0 - n_state + lax.broadcasted_iota(jnp.int32, (t, n_state), 1))
    smask = ((sdiff >= 0) & (sdiff < WINDOW))[None]
    ndiff = tcol - lax.broadcasted_iota(jnp.int32, (t, NEW_PAD), 1)
    nmask = (ndiff >= 0)[None]
    wst = wst_ref[0].astype(BF16)
    wnew = _pad_rows(wnew_ref[0], NEW_PAD).astype(BF16)
    for g in range(N_KV_HEADS):
        lo, hi = g * HEAD_DIM, (g + 1) * HEAD_DIM
        qg = q_ref[0, g * Q_PER_KV:(g + 1) * Q_PER_KV].reshape(Q_PER_KV * t, HEAD_DIM).astype(BF16)
        o_cmp, imp = _cmp_branch(qg, kc_ref[0][:, lo:hi], vc_ref[0][:, lo:hi], cmask, ov_ref[...])
        sel = _select(imp, qpos // SLC_BLOCK, n_sel)
        sel_ref[0, g] = jnp.where(sel, 1.0, 0.0)
        ocmp_ref[0, g] = o_cmp
        carry = (jnp.full((Q_PER_KV, t, 1), NEG, F32), jnp.zeros((Q_PER_KV, t, 1), F32),
                 jnp.zeros((Q_PER_KV, t, HEAD_DIM), F32))
        s3 = _dot_nt(qg, wst[:, lo:hi]).reshape(Q_PER_KV, t, n_state)
        carry = _online_step(carry, s3, smask, wst[:, KV_W + lo:KV_W + hi])
        s3 = _dot_nt(qg, wnew[:, lo:hi]).reshape(Q_PER_KV, t, NEW_PAD)
        _, l, acc = _online_step(carry, s3, nmask, wnew[:, KV_W + lo:KV_W + hi])
        owin_ref[0, g] = (acc / l).reshape(Q_PER_KV * t, HEAD_DIM)


def _attn_sample_a(q, kc, vc, win_state, win_new, ov, pos0, n_cmp, n_sel):
    bx, _, t, _ = q.shape
    nc = kc.shape[1]
    n_state = win_state.shape[1]
    lanes = ov.shape[1]
    rows = Q_PER_KV * t
    return pl.pallas_call(
        functools.partial(_attn_sample_a_kernel, pos0=pos0, n_cmp=n_cmp, n_sel=n_sel),
        grid=(bx,),
        in_specs=[pl.BlockSpec((1, N_Q_HEADS, t, HEAD_DIM), lambda b: (b, 0, 0, 0)),
                  pl.BlockSpec((1, nc, KV_W), lambda b: (b, 0, 0)),
                  pl.BlockSpec((1, nc, KV_W), lambda b: (b, 0, 0)),
                  pl.BlockSpec((1, n_state, 2 * KV_W), lambda b: (b, 0, 0)),
                  pl.BlockSpec((1, t, 2 * KV_W), lambda b: (b, 0, 0)),
                  pl.BlockSpec(ov.shape, lambda b: (0, 0))],
        out_specs=[pl.BlockSpec((1, N_KV_HEADS, rows, HEAD_DIM), lambda b: (b, 0, 0, 0)),
                   pl.BlockSpec((1, N_KV_HEADS, rows, HEAD_DIM), lambda b: (b, 0, 0, 0)),
                   pl.BlockSpec((1, N_KV_HEADS, t, lanes), lambda b: (b, 0, 0, 0))],
        out_shape=[jax.ShapeDtypeStruct((bx, N_KV_HEADS, rows, HEAD_DIM), F32),
                   jax.ShapeDtypeStruct((bx, N_KV_HEADS, rows, HEAD_DIM), F32),
                   jax.ShapeDtypeStruct((bx, N_KV_HEADS, t, lanes), F32)],
        compiler_params=_params("parallel"),
        name="attn_sample_a",
    )(q, kc, vc, win_state, win_new, ov)


def _attn_sample_b_kernel(pt_ref, *refs, pps, pos0):
    pages = refs[:pps]
    (q_ref, kvn_ref, sel_ref, ocmp_ref, owin_ref, gates_ref, o_ref,
     kbuf_ref, m_ref, l_ref, acc_ref) = refs[pps:]
    t = q_ref.shape[2]
    page = pages[0].shape[2]
    n = pps * page
    s = pl.program_id(1)

    @pl.when(s == 0)
    def _():
        m_ref[...] = jnp.full(m_ref.shape, NEG, F32)
        l_ref[...] = jnp.zeros(l_ref.shape, F32)
        acc_ref[...] = jnp.zeros(acc_ref.shape, F32)

    for k in range(pps):
        kbuf_ref[k * page:(k + 1) * page, :] = pages[k][0, 0].astype(BF16)
    k0 = s * n
    kbuf = kbuf_ref[...]
    qs = []
    for g in range(N_KV_HEADS):
        lo, hi = g * HEAD_DIM, (g + 1) * HEAD_DIM
        qg = q_ref[0, g * Q_PER_KV:(g + 1) * Q_PER_KV].reshape(Q_PER_KV * t, HEAD_DIM).astype(BF16)
        qs.append(qg)
        sel_bf = sel_ref[0, g].astype(BF16)
        mask = (_expand_blocks(sel_bf, k0, n) > 0.5)[None]
        s3 = _dot_nt(qg, kbuf[:, lo:hi]).reshape(Q_PER_KV, t, n)
        carry = _online_step((m_ref[g], l_ref[g], acc_ref[g]), s3, mask, kbuf[:, KV_W + lo:KV_W + hi])
        m_ref[g], l_ref[g], acc_ref[g] = carry

    @pl.when(s == pl.num_programs(1) - 1)
    def _():
        gates = gates_ref[0]
        kvn = _pad_rows(kvn_ref[0], NEW_PAD).astype(BF16)
        tcol = lax.broadcasted_iota(jnp.int32, (t, 1), 0)
        kidx = lax.broadcasted_iota(jnp.int32, (t, NEW_PAD), 1)
        heads = []
        for g in range(N_KV_HEADS):
            lo, hi = g * HEAD_DIM, (g + 1) * HEAD_DIM
            sel_bf = sel_ref[0, g].astype(BF16)
            mask = ((_expand_blocks(sel_bf, pos0, NEW_PAD) > 0.5) & (kidx <= tcol))[None]
            s3 = _dot_nt(qs[g], kvn[:, lo:hi]).reshape(Q_PER_KV, t, NEW_PAD)
            _, l, acc = _online_step((m_ref[g], l_ref[g], acc_ref[g]), s3, mask, kvn[:, KV_W + lo:KV_W + hi])
            o_slc = acc / l
            heads += _gate_combine(gates, g, ocmp_ref[0, g].reshape(Q_PER_KV, t, HEAD_DIM), o_slc,
                                   owin_ref[0, g].reshape(Q_PER_KV, t, HEAD_DIM))
        o_ref[0] = jnp.concatenate(heads, axis=-1)


def _attn_sample_b(cache4, layer, page_table, q, kv_new, sel, o_cmp, o_win, gates, pps, pos0):
    bx, n_pages = page_table.shape
    page = cache4.shape[2]
    t = q.shape[2]
    lanes = sel.shape[-1]
    rows = Q_PER_KV * t
    page_specs = [pl.BlockSpec((1, 1, page, 2 * KV_W),
                               lambda b, s, pt, k=k: (layer, pt[b, s * pps + k], 0, 1))
                  for k in range(pps)]
    per_b4 = lambda shp: pl.BlockSpec((1,) + shp, lambda b, s, pt: (b, 0, 0, 0))
    return pl.pallas_call(
        functools.partial(_attn_sample_b_kernel, pps=pps, pos0=pos0),
        grid_spec=pltpu.PrefetchScalarGridSpec(
            num_scalar_prefetch=1,
            grid=(bx, n_pages // pps),
            in_specs=page_specs + [
                per_b4((N_Q_HEADS, t, HEAD_DIM)),
                pl.BlockSpec((1, t, 2 * KV_W), lambda b, s, pt: (b, 0, 1)),
                per_b4((N_KV_HEADS, t, lanes)),
                per_b4((N_KV_HEADS, rows, HEAD_DIM)),
                per_b4((N_KV_HEADS, rows, HEAD_DIM)),
                pl.BlockSpec((1, t, GATE_PAD), lambda b, s, pt: (b, 0, 0))],
            out_specs=pl.BlockSpec((1, t, ATTN_W), lambda b, s, pt: (b, 0, 0)),
            scratch_shapes=[pltpu.VMEM((pps * page, 2 * KV_W), BF16),
                            pltpu.VMEM((N_KV_HEADS, Q_PER_KV, t, 1), F32),
                            pltpu.VMEM((N_KV_HEADS, Q_PER_KV, t, 1), F32),
                            pltpu.VMEM((N_KV_HEADS, Q_PER_KV, t, HEAD_DIM), F32)]),
        out_shape=jax.ShapeDtypeStruct((bx, t, ATTN_W), F32),
        compiler_params=_params("parallel", "arbitrary"),
        name="attn_sample_b",
    )(page_table, *([cache4] * pps), q, kv_new, sel, o_cmp, o_win, gates)


def _mix_out_kernel(x_ref, mod_ref, oattn_ref, bg_ref, u_ref, uh_ref, p_ref, ph_ref, cw_ref, cb_ref,
                    pw_ref, ps_ref, wo_ref, norm_ref, x1_ref, h2_ref, *, pos0, zero_first_halo):
    bb, tt, d = x_ref.shape
    rows = bb * tt
    i = pl.program_id(1)
    uh = uh_ref[...]
    ph = ph_ref[...]
    if zero_first_halo:
        keep = jnp.where(i > 0, 1.0, 0.0)
        uh = uh * keep
        ph = ph * keep
    ucat = jnp.concatenate([uh, u_ref[...]], axis=1)
    conv = cb_ref[...]
    for j in range(CONV_K):
        off = CONV_HALO - CONV_BUF + j
        conv = conv + ucat[:, off:off + tt] * cw_ref[j:j + 1]
    y_conv = bg_ref[...] * conv
    pcat = jnp.concatenate([ph, p_ref[...]], axis=1)
    a2 = pcat[:, 1:] + pcat[:, :-1]
    a4 = a2[:, 2:] + a2[:, :-2]
    a8 = a4[:, 4:] + a4[:, :-4]
    a16 = a8[:, 8:] + a8[:, :-8]
    sums = (a2[:, POOL_HALO - 1:POOL_HALO - 1 + tt], a4[:, POOL_HALO - 3:POOL_HALO - 3 + tt],
            a8[:, POOL_HALO - 7:POOL_HALO - 7 + tt], a16[:, POOL_HALO - 15:POOL_HALO - 15 + tt])
    pos = pos0 + i * tt + lax.broadcasted_iota(jnp.int32, (1, tt, 1), 1)
    grp = lax.broadcasted_iota(jnp.int32, (1, 1, POOL_W), 2) // (POOL_W // len(POOL_WINDOWS))
    mean = jnp.zeros((bb, tt, POOL_W), F32)
    for gi, w in enumerate(POOL_WINDOWS):
        cnt = jnp.minimum(w, pos + 1).astype(F32)
        mean = jnp.where(grp == gi, sums[gi] / cnt, mean)
    dlt = (mean - p_ref[...]).reshape(rows, POOL_W).astype(BF16)
    y_pool = _dot(dlt, pw_ref[...]) * ps_ref[...]
    mix = _dot(oattn_ref[...].reshape(rows, ATTN_W).astype(BF16), wo_ref[0:ATTN_W])
    mix = mix + _dot(y_conv.reshape(rows, CONV_W).astype(BF16), wo_ref[ATTN_W:ATTN_W + CONV_W])
    mix = mix + _dot(y_pool.astype(BF16), wo_ref[ATTN_W + CONV_W:ATTN_W + CONV_W + POOL_W])
    gate1 = mod_ref[:, :, 2 * d:3 * d]
    x1 = x_ref[...] + gate1 * mix.reshape(bb, tt, d)
    x1_ref[...] = x1
    y = x1 * lax.rsqrt(jnp.mean(x1 * x1, axis=-1, keepdims=True) + EPS) * norm_ref[...]
    h2_ref[...] = (y * (1.0 + mod_ref[:, :, 4 * d:5 * d]) + mod_ref[:, :, 3 * d:4 * d]).astype(h2_ref.dtype)


def _mix_out(x, mod, oattn, bg, u, u_halo, pool, p_halo, conv_w, conv_b, pool_wbd, pool_scale, w_out, norm2,
             bb, tt, pos0, halo_from_self):
    bx, t, d = x.shape
    blk = lambda w: pl.BlockSpec((bb, tt, w), lambda b, i: (b, i, 0))
    full = lambda a: pl.BlockSpec(a.shape, lambda b, i: (0,) * a.ndim)
    if halo_from_self:
        uh_spec = pl.BlockSpec((bb, CONV_HALO, CONV_W),
                               lambda b, i: (b, jnp.maximum(i * (tt // CONV_HALO) - 1, 0), 0))
        ph_spec = pl.BlockSpec((bb, POOL_HALO, POOL_W),
                               lambda b, i: (b, jnp.maximum(i * (tt // POOL_HALO) - 1, 0), 0))
    else:
        uh_spec = pl.BlockSpec((bb, CONV_HALO, CONV_W), lambda b, i: (b, 0, 0))
        ph_spec = pl.BlockSpec((bb, POOL_HALO, POOL_W), lambda b, i: (b, 0, 0))
    return pl.pallas_call(
        functools.partial(_mix_out_kernel, pos0=pos0, zero_first_halo=halo_from_self),
        grid=(bx // bb, t // tt),
        in_specs=[blk(d), pl.BlockSpec((bb, 1, mod.shape[-1]), lambda b, i: (b, 0, 0)),
                  blk(ATTN_W), blk(CONV_W), blk(CONV_W), uh_spec, blk(POOL_W), ph_spec,
                  full(conv_w), full(conv_b), full(pool_wbd), full(pool_scale), full(w_out), full(norm2)],
        out_specs=[blk(d), blk(d)],
        out_shape=[jax.ShapeDtypeStruct((bx, t, d), F32), jax.ShapeDtypeStruct((bx, t, d), oattn.dtype)],
        compiler_params=_params("parallel", "parallel"),
        name="mix_out",
    )(x, mod, oattn, bg, u, u_halo, pool, p_halo, conv_w, conv_b, pool_wbd, pool_scale, w_out, norm2)


def _ffn_kernel(x1_ref, h2_ref, mod_ref, wa_ref, wb_ref, wd_ref, o_ref, acc_ref):
    bb, tt, d = x1_ref.shape
    j = pl.program_id(2)

    @pl.when(j == 0)
    def _():
        acc_ref[...] = jnp.zeros(acc_ref.shape, F32)

    h = h2_ref[...].reshape(bb * tt, d).astype(BF16)
    a = _dot(h, wa_ref[...])
    b = _dot(h, wb_ref[...])
    acc_ref[...] += _dot((_silu(a) * b).astype(BF16), wd_ref[...])

    @pl.when(j == pl.num_programs(2) - 1)
    def _():
        o_ref[...] = x1_ref[...] + mod_ref[:, :, 5 * d:6 * d] * acc_ref[...].reshape(bb, tt, d)


def _ffn(x1, h2, mod, w_up, w_down, bb, tt, th):
    bx, t, d = x1.shape
    hid = w_down.shape[0]
    nh = hid // th
    blk = pl.BlockSpec((bb, tt, d), lambda b, i, j: (b, i, 0))
    return pl.pallas_call(
        _ffn_kernel,
        grid=(bx // bb, t // tt, nh),
        in_specs=[blk, blk, pl.BlockSpec((bb, 1, mod.shape[-1]), lambda b, i, j: (b, 0, 0)),
                  pl.BlockSpec((d, th), lambda b, i, j: (0, j)),
                  pl.BlockSpec((d, th), lambda b, i, j: (0, nh + j)),
                  pl.BlockSpec((th, d), lambda b, i, j: (j, 0))],
        out_specs=blk,
        out_shape=jax.ShapeDtypeStruct((bx, t, d), F32),
        scratch_shapes=[pltpu.VMEM((bb * tt, d), F32)],
        compiler_params=_params("parallel", "parallel", "arbitrary"),
        name="ffn",
    )(x1, h2, mod, w_up, w_up, w_down)


def _overlap_matrix(n_cmp, n_sel, rows, lanes):
    m = np.zeros((rows, lanes), np.float32)
    i = np.arange(n_cmp)
    for part in range(CMP_BLOCK // CMP_STRIDE):
        j = np.minimum((i + part) * CMP_STRIDE // SLC_BLOCK, n_sel - 1)
        np.add.at(m, (i, j), 1.0)
    return jnp.asarray(m, BF16)


def _seg_matrix(n):
    idx = np.arange(n) // HEAD_DIM
    return jnp.asarray((idx[:, None] == idx[None, :]).astype(np.float32) / HEAD_DIM, BF16)


def _prep_weights(w_in, w_out, q_norm, k_norm, cmp_pe, cmp_w1, cmp_w2, pool_w, w_up, w_down):
    depth, d, _ = w_in.shape
    g0 = ATTN_W + 6 * KV_W
    w_in_r = jnp.concatenate([w_in[:, :, :g0], w_in[:, :, g0 + N_Q_HEADS * N_BRANCH:],
                              w_in[:, :, g0:g0 + N_Q_HEADS * N_BRANCH],
                              jnp.zeros((depth, d, GATE_PAD - N_Q_HEADS * N_BRANCH), w_in.dtype)], axis=-1)
    w1r = cmp_w1.reshape(depth, 2, 2, CMP_STRIDE, HEAD_DIM, HEAD_DIM)
    wr = jnp.zeros((depth, CMP_STRIDE, 2, KV_W, 2 * KV_W), F32)
    w2bd = jnp.zeros((depth, 2, KV_W, KV_W), F32)
    for s in range(2):
        for g in range(N_KV_HEADS):
            r0 = g * HEAD_DIM
            w2bd = w2bd.at[:, s, r0:r0 + HEAD_DIM, r0:r0 + HEAD_DIM].set(cmp_w2[:, s])
            for half in range(2):
                c0 = half * KV_W + r0
                wr = wr.at[:, :, s, r0:r0 + HEAD_DIM, c0:c0 + HEAD_DIM].set(w1r[:, s, half])
    pe = cmp_pe.reshape(depth, 2, 2, CMP_STRIDE, 1, HEAD_DIM)
    pe = jnp.broadcast_to(pe, (depth, 2, 2, CMP_STRIDE, N_KV_HEADS, HEAD_DIM))
    pe = jnp.transpose(pe, (0, 3, 1, 2, 4, 5)).reshape(depth, CMP_STRIDE, 2, 2, KV_W)
    pe_aug = jnp.concatenate([pe, jnp.zeros((depth, CMP_STRIDE, 2, 6, KV_W), F32)], axis=3)
    gw = POOL_W // len(POOL_WINDOWS)
    pool_bd = jnp.zeros((depth, POOL_W, POOL_W), F32)
    for g in range(len(POOL_WINDOWS)):
        pool_bd = pool_bd.at[:, g * gw:(g + 1) * gw, g * gw:(g + 1) * gw].set(pool_w[:, g])
    return dict(
        w_in=w_in_r.astype(BF16), w_out=w_out.astype(BF16), w_up=w_up.astype(BF16), w_down=w_down.astype(BF16),
        wr=wr.astype(BF16), pe_aug=pe_aug, w2bd=w2bd.astype(BF16), pool_bd=pool_bd.astype(BF16),
        q_gain=jnp.tile(q_norm, (1, N_Q_HEADS))[:, None, :],
        k_gain=jnp.tile(k_norm, (1, 1, N_KV_HEADS)),
    )


def _token_tiles(t):
    return (1, min(t, 512)) if t >= 128 else None


def _layer_prompt(x, mod, l, W, P, consts):
    bx, t, d = x.shape
    bb, tt = 1, min(t, 512)
    kg = P["k_gain"][l]
    q, kv, win, bg, u, pool, gates, kvs_bf, win_bf = _proj_in(
        x, mod, W["norm_mix"][l][None, None], P["w_in"][l], P["q_gain"][l], kg[1:3],
        consts["seg_q"], consts["seg_k"], bb, tt, True)
    kc, vc = _cmp_prompt(kv, P["wr"][l], P["pe_aug"][l], P["w2bd"][l], consts["seg_k"], kg[0:1])
    oattn = _attn_prompt(q, kc, vc, kvs_bf, win_bf, gates, consts["ov_prompt"], 128, 512)
    x1, h2 = _mix_out(x, mod, oattn, bg, u, u, pool, pool, W["conv_w"][l], W["conv_bias"][l][None],
                      P["pool_bd"][l], W["pool_scale"][l][None], P["w_out"][l], W["norm_ffn"][l][None, None],
                      bb, tt, 0, True)
    x2 = _ffn(x1, h2, mod, P["w_up"][l], P["w_down"][l], bb, tt, consts["th"])
    states = (kv.reshape(bx, t, 4, N_KV_HEADS, HEAD_DIM),
              win[:, t - min(WINDOW, t):].reshape(bx, min(WINDOW, t), 2, N_KV_HEADS, HEAD_DIM),
              u[:, t - CONV_BUF:], pool[:, t - POOL_BUF:])
    return x2, states


def _layer_sample(x, mod, l, W, P, consts, cache4, page_table, win_state, conv_state, pool_state):
    bx, t, d = x.shape
    n_pages = page_table.shape[1]
    page = cache4.shape[2]
    pos0 = n_pages * page
    pps = consts["pps"]
    kg = P["k_gain"][l]
    q, kv, win, bg, u, pool, gates = _proj_in(
        x, mod, W["norm_mix"][l][None, None], P["w_in"][l], P["q_gain"][l], kg[1:3],
        consts["seg_q"], consts["seg_k"], bx, t, False)
    kc, vc = _cmp_sample(cache4, l, page_table, P["wr"][l], P["pe_aug"][l], P["w2bd"][l],
                         consts["seg_k"], kg[0:1], pps)
    n_state = win_state.shape[1]
    wst = win_state.reshape(bx, n_state, 2 * KV_W)
    o_cmp, o_win, sel = _attn_sample_a(q, kc, vc, wst, win, consts["ov_sample"], pos0,
                                       consts["n_cmp_s"], consts["n_sel_s"])
    oattn = _attn_sample_b(cache4, l, page_table, q, kv, sel, o_cmp, o_win, gates, pps, pos0)
    u_halo = jnp.concatenate([jnp.zeros((bx, CONV_HALO - CONV_BUF, CONV_W), F32), conv_state], axis=1)
    p_halo = jnp.concatenate([jnp.zeros((bx, POOL_HALO - POOL_BUF, POOL_W), F32), pool_state], axis=1)
    x1, h2 = _mix_out(x, mod, oattn, bg, u, u_halo, pool, p_halo, W["conv_w"][l], W["conv_bias"][l][None],
                      P["pool_bd"][l], W["pool_scale"][l][None], P["w_out"][l], W["norm_ffn"][l][None, None],
                      bx, t, pos0, False)
    x2 = _ffn(x1, h2, mod, P["w_up"][l], P["w_down"][l], bx, t, consts["th"])
    win_ext = jnp.concatenate([wst, win], axis=1)
    states = (kv.reshape(bx, t, 4, N_KV_HEADS, HEAD_DIM),
              win_ext[:, t:].reshape(bx, n_state, 2, N_KV_HEADS, HEAD_DIM),
              jnp.concatenate([conv_state, u], axis=1)[:, t:],
              jnp.concatenate([pool_state, pool], axis=1)[:, t:])
    return x2, states


def kernel(x_prompt, x_sample, cache_nsa_kv, state_win_kv, state_conv, state_pool, page_table, c_prompt, c_sample, norm_mix, norm_ffn, w_ada, b_ada, w_in, w_out, q_norm, k_norm, cmp_pe, cmp_w1, cmp_w2, conv_w, conv_bias, pool_w, pool_scale, w_up, w_down):
    depth = w_in.shape[0]
    bp, tp, d = x_prompt.shape
    bs, ts, _ = x_sample.shape
    n_pages = page_table.shape[1]
    n_pool, page = cache_nsa_kv.shape[1:3]
    past = n_pages * page
    hid = w_down.shape[1]

    W = dict(norm_mix=norm_mix, norm_ffn=norm_ffn, conv_w=conv_w, conv_bias=conv_bias, pool_scale=pool_scale)
    P = _prep_weights(w_in, w_out, q_norm, k_norm, cmp_pe, cmp_w1, cmp_w2, pool_w, w_up, w_down)
    n_cmp_s = (past + ts - CMP_BLOCK) // CMP_STRIDE + 1
    n_sel_s = -(-(past + ts) // SLC_BLOCK)
    sel_lanes = -(-n_sel_s // 128) * 128
    consts = dict(
        seg_q=_seg_matrix(ATTN_W), seg_k=_seg_matrix(KV_W),
        ov_prompt=_overlap_matrix((tp - CMP_BLOCK) // CMP_STRIDE + 1, tp // SLC_BLOCK, tp // CMP_STRIDE, 128),
        ov_sample=_overlap_matrix(n_cmp_s, n_sel_s, past // CMP_STRIDE, sel_lanes),
        n_cmp_s=n_cmp_s, n_sel_s=n_sel_s, pps=min(16, n_pages),
        th=hid // 2 if (hid // 2) % 128 == 0 else hid,
    )
    cache4 = cache_nsa_kv.reshape(depth, n_pool, page, 4 * KV_W)
    mod = _ada(jnp.concatenate([c_prompt, c_sample], axis=0), w_ada, b_ada)
    mod_p = mod[:, :bp, None, :]
    mod_s = mod[:, bp:, None, :]

    yp, ys = x_prompt, x_sample
    outs_p, outs_s = [], []
    for l in range(depth):
        yp, st = _layer_prompt(yp, mod_p[l], l, W, P, consts)
        outs_p.append(st)
        ys, st = _layer_sample(ys, mod_s[l], l, W, P, consts, cache4, page_table,
                               state_win_kv[l], state_conv[l], state_pool[l])
        outs_s.append(st)
    stack = lambda outs, k: jnp.stack([o[k] for o in outs])
    return (yp, ys, stack(outs_p, 0), stack(outs_s, 0), stack(outs_p, 1), stack(outs_s, 1),
            stack(outs_p, 2), stack(outs_s, 2), stack(outs_p, 3), stack(outs_s, 3))
```

```python
import functools

import numpy as np
import jax
import jax.numpy as jnp
from jax import lax
from jax.experimental import pallas as pl
from jax.experimental.pallas import tpu as pltpu

F32 = jnp.float32
BF16 = jnp.bfloat16

HEAD_DIM = 64
N_Q_HEADS = 8
N_KV_HEADS = 2
Q_PER_KV = N_Q_HEADS // N_KV_HEADS
N_BRANCH = 3
ATTN_W = N_Q_HEADS * HEAD_DIM
KV_W = N_KV_HEADS * HEAD_DIM
CONV_W = 256
POOL_W = 256
CONV_K = 3
CONV_BUF = CONV_K - 1
POOL_WINDOWS = (2, 4, 8, 16)
POOL_BUF = max(POOL_WINDOWS) - 1
CMP_BLOCK = 32
CMP_STRIDE = 16
SLC_BLOCK = 64
N_SLC = 16
WINDOW = 512
EPS = 1e-6
NEG = -1e30
FORCE = 1e4
ATTN_SCALE = HEAD_DIM ** -0.5
LOG2E = 1.4426950408889634
LANE = 128
GATE_PAD = LANE
IN_W_PAD = ATTN_W + 4 * KV_W + 2 * KV_W + 3 * CONV_W + POOL_W + GATE_PAD
CONV_HALO = 8
POOL_HALO = 16
NEW_PAD = LANE
VMEM_LIMIT = 56 * 1024 * 1024
ROW_TILE = 512
TQ = 128
SLC_CHUNK = ROW_TILE
PAGES_PER_STEP = 16
ADA_TN = 1536

_NT = (((1,), (1,)), ((), ()))


def _params(*sem):
    return pltpu.CompilerParams(dimension_semantics=sem, vmem_limit_bytes=VMEM_LIMIT)


def _dot(a, b):
    return jnp.dot(a, b, preferred_element_type=F32)


def _dot_nt(a, b):
    return lax.dot_general(a, b, _NT, preferred_element_type=F32)


def _split_hilo(x):
    hi = x.astype(BF16)
    return hi, (x - hi.astype(F32)).astype(BF16)


def _dot_hilo(x, w_bf):
    hi, lo = _split_hilo(x)
    return _dot(hi, w_bf) + _dot(lo, w_bf)


def _seg_rms(x, seg_bf, gain):
    ms = _dot_hilo(x * x, seg_bf)
    return x * lax.rsqrt(ms + EPS) * gain


def _silu(x):
    return x * jax.nn.sigmoid(x)


def _pad_rows(x, n):
    return jnp.concatenate([x, jnp.zeros((n - x.shape[0], x.shape[1]), x.dtype)], axis=0)


def _iota(shape, axis):
    return lax.broadcasted_iota(jnp.int32, shape, axis)


def _ada_kernel(c_ref, w_ref, b_ref, o_ref):
    c = c_ref[...]
    o_ref[0] = _dot(_silu(c).astype(BF16), w_ref[0].astype(BF16)) + b_ref[0]


def _ada(c_all, w_ada, b_ada):
    depth, d, n = w_ada.shape
    r = c_all.shape[0]
    return pl.pallas_call(
        _ada_kernel,
        grid=(depth, n // ADA_TN),
        in_specs=[pl.BlockSpec((r, d), lambda l, j: (0, 0)),
                  pl.BlockSpec((1, d, ADA_TN), lambda l, j: (l, 0, j)),
                  pl.BlockSpec((1, 1, ADA_TN), lambda l, j: (l, 0, j))],
        out_specs=pl.BlockSpec((1, r, ADA_TN), lambda l, j: (l, 0, j)),
        out_shape=jax.ShapeDtypeStruct((depth, r, n), F32),
        compiler_params=_params("parallel", "parallel"),
        name="ada",
    )(c_all, w_ada, b_ada.reshape(depth, 1, n))


def _proj_in_kernel(x_ref, mod_ref, norm_ref, w_ref, qg_ref, kg_ref, seg_q_ref, seg_k_ref,
                    q_ref, bg_ref, u_ref, pool_ref, gates_ref, *kv_refs, feature_major):
    bb, tt, d = x_ref.shape
    rows = bb * tt
    x = x_ref[...]
    y = x * lax.rsqrt(jnp.mean(x * x, axis=-1, keepdims=True) + EPS) * norm_ref[...]
    shift1 = mod_ref[:, :, 0:d]
    scale1 = mod_ref[:, :, d:2 * d]
    h = (y * (1.0 + scale1) + shift1).reshape(rows, d).astype(BF16)
    z = _dot(h, w_ref[...])

    qn = _seg_rms(z[:, 0:ATTN_W], seg_q_ref[...], qg_ref[...]) * (ATTN_SCALE * LOG2E)
    q_ref[...] = qn.reshape(bb, tt, ATTN_W)
    o = ATTN_W
    raw_cmp = z[:, o:o + 2 * KV_W]
    ks = _seg_rms(z[:, o + 2 * KV_W:o + 3 * KV_W], seg_k_ref[...], kg_ref[0:1])
    vs = z[:, o + 3 * KV_W:o + 4 * KV_W]
    kv = jnp.concatenate([raw_cmp, ks, vs], axis=-1)
    o += 4 * KV_W
    kw = _seg_rms(z[:, o:o + KV_W], seg_k_ref[...], kg_ref[1:2])
    vw = z[:, o + KV_W:o + 2 * KV_W]
    win = jnp.concatenate([kw, vw], axis=-1)
    o += 2 * KV_W
    if feature_major:
        kvt_ref, wint_ref, raw_ref, kst_ref, vs_ref, kwt_ref, vw_ref = kv_refs
        kvt_ref[0] = kv.T
        wint_ref[0] = win.T
        raw_ref[0] = raw_cmp
        kst_ref[0, 0] = ks.T.astype(BF16)
        vs_ref[0] = vs.astype(BF16)
        kwt = kw.T.astype(BF16)
        for j in range(rows // LANE):
            kwt_ref[0, j] = kwt[:, j * LANE:(j + 1) * LANE]
        vw_ref[0] = vw.astype(BF16)
    else:
        kv_ref, win_ref = kv_refs
        kv_ref[...] = kv.reshape(bb, tt, 4 * KV_W)
        win_ref[...] = win.reshape(bb, tt, 2 * KV_W)
    bg_ref[...] = z[:, o:o + CONV_W].reshape(bb, tt, CONV_W)
    u_ref[...] = (z[:, o + CONV_W:o + 2 * CONV_W] * z[:, o + 2 * CONV_W:o + 3 * CONV_W]).reshape(bb, tt, CONV_W)
    o += 3 * CONV_W
    pool_ref[...] = z[:, o:o + POOL_W].reshape(bb, tt, POOL_W)
    o += POOL_W
    gates_ref[...] = jax.nn.sigmoid(z[:, o:o + GATE_PAD]).reshape(bb, tt, GATE_PAD)


def _proj_in(x, mod, norm1, w_in, q_gain, k_gain2, seg_q, seg_k, bb, tt, feature_major):
    bx, t, d = x.shape
    grid = (bx // bb, t // tt)
    blk = lambda w: pl.BlockSpec((bb, tt, w), lambda b, i: (b, i, 0))
    full = lambda a: pl.BlockSpec(a.shape, lambda b, i: (0,) * a.ndim)
    outs = [((bx, t, ATTN_W), F32), ((bx, t, CONV_W), F32), ((bx, t, CONV_W), F32), ((bx, t, POOL_W), F32),
            ((bx, t, GATE_PAD), F32)]
    out_specs = [blk(s[-1]) for s, _ in outs]
    if feature_major:
        assert bb == 1 and tt == SLC_CHUNK and tt % LANE == 0
        outs += [((bx, 4 * KV_W, t), F32), ((bx, 2 * KV_W, t), F32), ((bx, t, 2 * KV_W), F32),
                 ((bx, t // tt, KV_W, tt), BF16), ((bx, t, KV_W), BF16),
                 ((bx, t // LANE, KV_W, LANE), BF16), ((bx, t, KV_W), BF16)]
        out_specs += [pl.BlockSpec((1, 4 * KV_W, tt), lambda b, i: (b, 0, i)),
                      pl.BlockSpec((1, 2 * KV_W, tt), lambda b, i: (b, 0, i)),
                      blk(2 * KV_W),
                      pl.BlockSpec((1, 1, KV_W, tt), lambda b, i: (b, i, 0, 0)),
                      blk(KV_W),
                      pl.BlockSpec((1, tt // LANE, KV_W, LANE), lambda b, i: (b, i, 0, 0)),
                      blk(KV_W)]
    else:
        outs += [((bx, t, 4 * KV_W), F32), ((bx, t, 2 * KV_W), F32)]
        out_specs += [blk(4 * KV_W), blk(2 * KV_W)]
    return pl.pallas_call(
        functools.partial(_proj_in_kernel, feature_major=feature_major),
        grid=grid,
        in_specs=[blk(d), pl.BlockSpec((bb, 1, mod.shape[-1]), lambda b, i: (b, 0, 0)),
                  full(norm1), full(w_in), full(q_gain), full(k_gain2), full(seg_q), full(seg_k)],
        out_specs=out_specs,
        out_shape=[jax.ShapeDtypeStruct(s, dt) for s, dt in outs],
        compiler_params=_params("parallel", "parallel"),
        name="proj_in",
    )(x, mod, norm1, w_in, q_gain, k_gain2, seg_q, seg_k)


def _compress_core(chunks, nc, wr_ref, pe_ref, w2_ref, seg_k_ref, kg0_ref):
    outs = []
    for s in range(2):
        acc = _dot(chunks(s).astype(BF16), wr_ref[s])
        acc_pe = _dot(pe_ref[s].astype(BF16), wr_ref[s])
        bias = acc_pe[0:1, 0:KV_W] + acc_pe[1:2, KV_W:2 * KV_W]
        nxt = pltpu.roll(acc[:, KV_W:2 * KV_W], nc - 1, 0)
        pre = acc[:, 0:KV_W] + nxt + bias
        outs.append(_dot(_silu(pre).astype(BF16), w2_ref[s]))
    return _seg_rms(outs[0], seg_k_ref[...], kg0_ref[...]), outs[1]


def _cmp_prompt_kernel(k_ref, v_ref, wr_ref, pe_ref, w2_ref, seg_k_ref, kg0_ref, kc_ref, vc_ref):
    nc = kc_ref.shape[1]
    src = (k_ref, v_ref)
    read = lambda s: jnp.concatenate(
        [src[s][0, pl.ds(r, nc, stride=CMP_STRIDE), :] for r in range(CMP_STRIDE)], axis=1)
    kc, vc = _compress_core(read, nc, wr_ref, pe_ref, w2_ref, seg_k_ref, kg0_ref)
    kc_ref[0] = kc.astype(BF16)
    vc_ref[0] = vc.astype(BF16)


def _cmp_prompt(raw, wr, pe_aug, w2bd, seg_k, kg0):
    bx, t, _ = raw.shape
    nc = t // CMP_STRIDE
    full = lambda a: pl.BlockSpec(a.shape, lambda b: (0,) * a.ndim)
    return pl.pallas_call(
        _cmp_prompt_kernel,
        grid=(bx,),
        in_specs=[pl.BlockSpec((1, t, KV_W), lambda b: (b, 0, 0)),
                  pl.BlockSpec((1, t, KV_W), lambda b: (b, 0, 1)),
                  full(wr), full(pe_aug), full(w2bd), full(seg_k), full(kg0)],
        out_specs=[pl.BlockSpec((1, nc, KV_W), lambda b: (b, 0, 0))] * 2,
        out_shape=[jax.ShapeDtypeStruct((bx, nc, KV_W), BF16)] * 2,
        compiler_params=_params("parallel"),
        name="cmp_prompt",
    )(raw, raw, wr, pe_aug, w2bd, seg_k, kg0)


def _cmp_sample_kernel(pt_ref, *refs, pps):
    pages = refs[:pps]
    perm_ref, wr_ref, pe_ref, w2_ref, seg_k_ref, kg0_ref, kc_ref, vc_ref, xs_ref = refs[pps:]
    step = pl.program_id(1)
    cpp = pages[0].shape[3] // CMP_STRIDE
    for k in range(pps):
        z = _dot(pages[k][0, 0].astype(BF16), perm_ref[...])
        row0 = pl.multiple_of((step * pps + k) * cpp, cpp)
        for s in range(2):
            zt = z[s * KV_W:(s + 1) * KV_W].T
            for r in range(CMP_STRIDE):
                xs_ref[s, pl.ds(row0, cpp), r * KV_W:(r + 1) * KV_W] = zt[r * cpp:(r + 1) * cpp]

    @pl.when(step == pl.num_programs(1) - 1)
    def _():
        nc = kc_ref.shape[1]
        read = lambda s: xs_ref[s]
        kc, vc = _compress_core(read, nc, wr_ref, pe_ref, w2_ref, seg_k_ref, kg0_ref)
        kc_ref[0] = kc.astype(BF16)
        vc_ref[0] = vc.astype(BF16)


def _cmp_sample(cache_t, layer, page_table, perm, wr, pe_aug, w2bd, seg_k, kg0, pps):
    bx, n_pages = page_table.shape
    page = cache_t.shape[3]
    past = n_pages * page
    nc = past // CMP_STRIDE
    full = lambda a: pl.BlockSpec(a.shape, lambda b, s, pt: (0,) * a.ndim)
    page_specs = [pl.BlockSpec((1, 1, 2 * KV_W, page),
                               lambda b, s, pt, k=k: (layer, pt[b, s * pps + k], 0, 0))
                  for k in range(pps)]
    return pl.pallas_call(
        functools.partial(_cmp_sample_kernel, pps=pps),
        grid_spec=pltpu.PrefetchScalarGridSpec(
            num_scalar_prefetch=1,
            grid=(bx, n_pages // pps),
            in_specs=page_specs + [full(perm), full(wr), full(pe_aug), full(w2bd), full(seg_k), full(kg0)],
            out_specs=[pl.BlockSpec((1, nc, KV_W), lambda b, s, pt: (b, 0, 0))] * 2,
            scratch_shapes=[pltpu.VMEM((2, nc, CMP_STRIDE * KV_W), F32)]),
        out_shape=[jax.ShapeDtypeStruct((bx, nc, KV_W), BF16)] * 2,
        compiler_params=_params("parallel", "arbitrary"),
        name="cmp_sample",
    )(page_table, *([cache_t] * pps), perm, wr, pe_aug, w2bd, seg_k, kg0)


def _q_rows(q, lane):
    out = []
    for g in range(N_KV_HEADS):
        keep = (lane >= g * HEAD_DIM) & (lane < (g + 1) * HEAD_DIM)
        rows = []
        for r in range(Q_PER_KV):
            h = g * Q_PER_KV + r
            blk = q[:, (h // 2) * LANE:(h // 2 + 1) * LANE]
            if h % 2 != g:
                blk = pltpu.roll(blk, HEAD_DIM, 1)
            rows.append(jnp.where(keep, blk, 0.0))
        out.append(jnp.concatenate(rows, axis=0).astype(BF16))
    return out


def _assemble_heads(vals, lane):
    cols = []
    for k in range(N_Q_HEADS // 2):
        a, b = vals[2 * k], vals[2 * k + 1]
        if (2 * k) // Q_PER_KV == 1:
            a = pltpu.roll(a, HEAD_DIM, 1)
        else:
            b = pltpu.roll(b, HEAD_DIM, 1)
        cols.append(jnp.where(lane < HEAD_DIM, a, b))
    return jnp.concatenate(cols, axis=-1)


def _gate_combine(gates, h, o_cmp, o_slc, o_win):
    c = h * N_BRANCH
    return gates[:, c:c + 1] * o_cmp + gates[:, c + 1:c + 2] * o_slc + gates[:, c + 2:c + 3] * o_win


def _softmax_parts(s3):
    m = jnp.max(s3, axis=-1, keepdims=True)
    e = jnp.exp2(s3 - m)
    return e, jnp.sum(e, axis=-1, keepdims=True)


def _online_update(m, l, acc, s3, pv):
    m_new = jnp.maximum(m, jnp.max(s3, axis=-1, keepdims=True))
    alpha = jnp.exp2(m - m_new)
    e = jnp.exp2(s3 - m_new)
    return m_new, alpha * l + jnp.sum(e, axis=-1, keepdims=True), alpha * acc + pv(e)


def _select_rows(imp, cur, n_sel):
    jidx = _iota(imp.shape, 1)
    valid = jidx <= cur
    forced = valid & ((jidx == 0) | (jidx == cur) | (jidx == cur - 1))
    sc = jnp.where(forced, FORCE, jnp.where(valid, imp, NEG))
    rank = jnp.zeros(imp.shape, jnp.int32)
    for i in range(n_sel):
        col = sc[:, i:i + 1]
        beats = (col > sc) | ((col == sc) & (jidx > i))
        rank = rank + jnp.where(beats, 1, 0)
    return (rank < min(N_SLC, n_sel)) & valid


def _select_cols(imp_t, cur, n_sel):
    jidx = _iota(imp_t.shape, 0)
    valid = jidx <= cur
    forced = valid & ((jidx == 0) | (jidx == cur) | (jidx == cur - 1))
    sc = jnp.where(forced, FORCE, jnp.where(valid, imp_t, NEG))
    rank = jnp.zeros(imp_t.shape, jnp.int32)
    for i in range(n_sel):
        row = sc[i:i + 1, :]
        beats = (row > sc) | ((row == sc) & (jidx > i))
        rank = rank + jnp.where(beats, 1, 0)
    return jnp.where((rank < min(N_SLC, n_sel)) & valid, 1.0, 0.0)


def _attn_prompt_kernel(q_ref, kc_ref, vc_ref, kst_ref, vs_ref, kwt_ref, vw_ref, gates_ref, ovt_ref, e_ref,
                        o_ref):
    tq = q_ref.shape[1]
    nc = kc_ref.shape[1]
    chunk = kst_ref.shape[3]
    n_sel = ovt_ref.shape[0]
    rows = Q_PER_KV * tq
    t0 = pl.program_id(1) * tq
    lane = _iota((tq, LANE), 1)
    qpos = t0 + _iota((tq, 1), 0)
    qs = _q_rows(q_ref[0], lane)
    gates = gates_ref[0]
    cbias = jnp.where((_iota((tq, nc), 1) * CMP_STRIDE + CMP_BLOCK - 1) <= qpos, 0.0, NEG)
    cvalid = jnp.where(qpos >= CMP_BLOCK - 1, 1.0, 0.0)
    wtile0 = jnp.maximum(t0 - WINDOW, 0) // LANE
    wstart = pl.multiple_of(wtile0 * LANE, LANE)
    wn = WINDOW + tq
    wdiff = qpos - (wstart + _iota((tq, wn), 1))
    wbias = jnp.where((wdiff >= 0) & (wdiff < WINDOW), 0.0, NEG)
    cur_t = (t0 + _iota((1, tq), 1)) // SLC_BLOCK
    n_chunks = (t0 + tq + chunk - 1) // chunk
    groups = range(N_KV_HEADS)
    own_half = [(_iota((1, LANE), 1) // HEAD_DIM) == g for g in groups]
    with_ones = lambda v, g: jnp.where(own_half[g], v, 1.0)
    normalise = lambda acc: acc / pltpu.roll(acc.reshape(rows, LANE), HEAD_DIM, 1).reshape(acc.shape)

    o_cmp, selneg = [], []
    for g in groups:
        e, l = _softmax_parts(_dot_nt(qs[g], kc_ref[0]).reshape(Q_PER_KV, tq, nc) + cbias[None])
        p = e * (cvalid / l)
        o_cmp.append(_dot(p.reshape(rows, nc).astype(BF16), vc_ref[0]).reshape(Q_PER_KV, tq, LANE))
        hi, lo = _split_hilo(p[0] + p[1] + p[2] + p[3])
        imp_t = _dot_nt(ovt_ref[...], hi) + _dot_nt(ovt_ref[...], lo)
        sel_t = _select_cols(imp_t, cur_t, n_sel)
        sel = jnp.concatenate([sel_t, jnp.zeros((LANE - n_sel, tq), F32)], axis=0).T
        selneg.append(jnp.where(sel > 0.5, 0.0, NEG).astype(BF16))

    def body(c, carry):
        causal = c * chunk + _iota((tq, chunk), 1) <= qpos
        kt = kst_ref[0, c]
        v = vs_ref[0, pl.ds(pl.multiple_of(c * chunk, chunk), chunk), :]
        out = []
        for g in groups:
            m, acc = carry[g]
            bias = jnp.where(causal, _dot(selneg[g], e_ref[c]), NEG)
            s3 = _dot(qs[g], kt).reshape(Q_PER_KV, tq, chunk) + bias[None]
            m_new = jnp.maximum(m, jnp.max(s3, axis=-1, keepdims=True))
            w = jnp.exp2(s3 - m_new).reshape(rows, chunk).astype(BF16)
            acc = jnp.exp2(m - m_new) * acc + _dot(w, with_ones(v, g)).reshape(Q_PER_KV, tq, LANE)
            out.append((m_new, acc))
        return tuple(out)

    init = tuple((jnp.full((Q_PER_KV, tq, 1), NEG, F32), jnp.zeros((Q_PER_KV, tq, LANE), F32)) for g in groups)
    slc = lax.fori_loop(0, n_chunks, body, init)

    vw = vw_ref[0, pl.ds(wstart, wn), :]
    vals = []
    for g in groups:
        s = jnp.concatenate([_dot(qs[g], kwt_ref[0, wtile0 + j]) for j in range(wn // LANE)], axis=-1)
        s3 = s.reshape(Q_PER_KV, tq, wn) + wbias[None]
        w = jnp.exp2(s3 - jnp.max(s3, axis=-1, keepdims=True)).reshape(rows, wn).astype(BF16)
        o_win = normalise(_dot(w, with_ones(vw, g)).reshape(Q_PER_KV, tq, LANE))
        o_slc = normalise(slc[g][1])
        for r in range(Q_PER_KV):
            vals.append(_gate_combine(gates, g * Q_PER_KV + r, o_cmp[g][r], o_slc[r], o_win[r]))
    o_ref[0] = _assemble_heads(vals, lane).astype(BF16)


def _attn_prompt(q, kc, vc, kst, vs, kwt, vw, gates, ovt, expand):
    bx, t, _ = q.shape
    nc = kc.shape[1]
    whole = lambda a: pl.BlockSpec((1,) + a.shape[1:], lambda b, i: (b,) + (0,) * (a.ndim - 1))
    const = lambda a: pl.BlockSpec(a.shape, lambda b, i: (0,) * a.ndim)
    return pl.pallas_call(
        _attn_prompt_kernel,
        grid=(bx, t // TQ),
        in_specs=[pl.BlockSpec((1, TQ, ATTN_W), lambda b, i: (b, i, 0)),
                  whole(kc), whole(vc), whole(kst), whole(vs), whole(kwt), whole(vw),
                  pl.BlockSpec((1, TQ, GATE_PAD), lambda b, i: (b, i, 0)),
                  const(ovt), const(expand)],
        out_specs=pl.BlockSpec((1, TQ, ATTN_W), lambda b, i: (b, i, 0)),
        out_shape=jax.ShapeDtypeStruct((bx, t, ATTN_W), BF16),
        compiler_params=_params("parallel", "parallel"),
        name="attn_prompt",
    )(q, kc, vc, kst, vs, kwt, vw, gates, ovt, expand)


def _attn_sample_a_kernel(q_ref, kc_ref, vc_ref, wst_ref, wnew_ref, ov_ref, e_ref, ocmp_ref, owin_ref, bias_ref,
                          *, pos0, n_cmp, n_sel):
    t = q_ref.shape[1]
    nc = kc_ref.shape[1]
    n_state = wst_ref.shape[3]
    nr = N_Q_HEADS
    lane = _iota((t, LANE), 1)
    tcol = _iota((t, 1), 0)
    qpos = pos0 + tcol
    qbd = jnp.concatenate(_q_rows(q_ref[0], lane), axis=0)
    cidx = _iota((t, nc), 1)
    cbias = jnp.where((cidx < n_cmp) & ((cidx * CMP_STRIDE + CMP_BLOCK - 1) <= qpos), 0.0, NEG)
    cvalid = jnp.where(qpos >= CMP_BLOCK - 1, 1.0, 0.0)
    e, l = _softmax_parts(_dot_nt(qbd, kc_ref[0]).reshape(nr, t, nc) + cbias[None])
    p = e * (cvalid / l)
    ocmp_ref[0] = _dot(p.reshape(nr * t, nc).astype(BF16), vc_ref[0])
    p4 = p.reshape(N_KV_HEADS, Q_PER_KV, t, nc)
    psum = (p4[:, 0] + p4[:, 1] + p4[:, 2] + p4[:, 3]).reshape(N_KV_HEADS * t, nc)
    imp = _dot_hilo(psum, ov_ref[...])
    cur = jnp.concatenate([qpos // SLC_BLOCK] * N_KV_HEADS, axis=0)
    selneg = jnp.where(_select_rows(imp, cur, n_sel), 0.0, NEG).astype(BF16)
    bias_ref[0] = _dot(selneg, e_ref[...]).reshape(N_KV_HEADS, t, e_ref.shape[1])
    wst = wst_ref[0, 0]
    kwt = wst[0:KV_W].astype(BF16)
    vwt = wst[KV_W:2 * KV_W].astype(BF16)
    wnew = _pad_rows(wnew_ref[0], NEW_PAD).astype(BF16)
    sdiff = qpos - (pos0 - n_state + _iota((t, n_state), 1))
    sbias = jnp.where((sdiff >= 0) & (sdiff < WINDOW), 0.0, NEG)
    nbias = jnp.where(tcol - _iota((t, NEW_PAD), 1) >= 0, 0.0, NEG)
    s_a = _dot(qbd, kwt).reshape(nr, t, n_state) + sbias[None]
    s_b = _dot_nt(qbd, wnew[:, 0:KV_W]).reshape(nr, t, NEW_PAD) + nbias[None]
    m = jnp.maximum(jnp.max(s_a, axis=-1, keepdims=True), jnp.max(s_b, axis=-1, keepdims=True))
    e_a = jnp.exp2(s_a - m)
    e_b = jnp.exp2(s_b - m)
    l = jnp.sum(e_a, axis=-1, keepdims=True) + jnp.sum(e_b, axis=-1, keepdims=True)
    o = (_dot_nt(e_a.reshape(nr * t, n_state).astype(BF16), vwt)
         + _dot(e_b.reshape(nr * t, NEW_PAD).astype(BF16), wnew[:, KV_W:2 * KV_W]))
    owin_ref[0] = o / l.reshape(nr * t, 1)


def _attn_sample_a(q, kc, vc, win_state_t, layer, win_new, ov, expand, pos0, n_cmp, n_sel):
    bx, t, _ = q.shape
    nc = kc.shape[1]
    n_state = win_state_t.shape[3]
    kp = expand.shape[1]
    rows = N_Q_HEADS * t
    per_b = lambda a: pl.BlockSpec((1,) + a.shape[1:], lambda b: (b,) + (0,) * (a.ndim - 1))
    const = lambda a: pl.BlockSpec(a.shape, lambda b: (0,) * a.ndim)
    return pl.pallas_call(
        functools.partial(_attn_sample_a_kernel, pos0=pos0, n_cmp=n_cmp, n_sel=n_sel),
        grid=(bx,),
        in_specs=[per_b(q), per_b(kc), per_b(vc),
                  pl.BlockSpec((1, 1, 2 * KV_W, n_state), lambda b: (layer, b, 0, 0)),
                  per_b(win_new), const(ov), const(expand)],
        out_specs=[pl.BlockSpec((1, rows, LANE), lambda b: (b, 0, 0)),
                   pl.BlockSpec((1, rows, LANE), lambda b: (b, 0, 0)),
                   pl.BlockSpec((1, N_KV_HEADS, t, kp), lambda b: (b, 0, 0, 0))],
        out_shape=[jax.ShapeDtypeStruct((bx, rows, LANE), F32),
                   jax.ShapeDtypeStruct((bx, rows, LANE), F32),
                   jax.ShapeDtypeStruct((bx, N_KV_HEADS, t, kp), F32)],
        compiler_params=_params("parallel"),
        name="attn_sample_a",
    )(q, kc, vc, win_state_t, win_new, ov, expand)


def _attn_sample_b_kernel(pt_ref, *refs, pps):
    pages = refs[:pps]
    (q_ref, kvn_ref, bias_ref, biasn_ref, ocmp_ref, owin_ref, gates_ref, o_ref,
     kt_ref, vt_ref, m_ref, l_ref, acc_ref) = refs[pps:]
    t = q_ref.shape[1]
    page = pages[0].shape[3]
    n = pps * page
    nr = N_Q_HEADS
    step = pl.program_id(1)

    @pl.when(step == 0)
    def _():
        m_ref[...] = jnp.full(m_ref.shape, NEG, F32)
        l_ref[...] = jnp.zeros(l_ref.shape, F32)
        acc_ref[...] = jnp.zeros(acc_ref.shape, F32)

    for k in range(pps):
        kt_ref[:, k * page:(k + 1) * page] = pages[k][0, 0, 0:KV_W, :].astype(BF16)
        vt_ref[:, k * page:(k + 1) * page] = pages[k][0, 0, KV_W:2 * KV_W, :].astype(BF16)
    lane = _iota((t, LANE), 1)
    qbd = jnp.concatenate(_q_rows(q_ref[0], lane), axis=0)

    def update(s_flat, bias, pv):
        width = s_flat.shape[-1]
        s3 = (s_flat.reshape(N_KV_HEADS, Q_PER_KV, t, width) + bias[:, None]).reshape(nr, t, width)
        m_ref[...], l_ref[...], acc_ref[...] = _online_update(m_ref[...], l_ref[...], acc_ref[...], s3, pv)

    update(_dot(qbd, kt_ref[...]), bias_ref[0],
           lambda w: _dot_nt(w.reshape(nr * t, n).astype(BF16), vt_ref[...]).reshape(nr, t, LANE))

    @pl.when(step == pl.num_programs(1) - 1)
    def _():
        kvn = _pad_rows(kvn_ref[0], NEW_PAD).astype(BF16)
        causal = _iota((t, NEW_PAD), 1) <= _iota((t, 1), 0)
        update(_dot_nt(qbd, kvn[:, 0:KV_W]), jnp.where(causal[None], biasn_ref[0], NEG),
               lambda w: _dot(w.reshape(nr * t, NEW_PAD).astype(BF16), kvn[:, KV_W:2 * KV_W]).reshape(nr, t, LANE))
        o_slc = acc_ref[...] / l_ref[...]
        gates = gates_ref[0]
        vals = [_gate_combine(gates, h, ocmp_ref[0, h * t:(h + 1) * t], o_slc[h], owin_ref[0, h * t:(h + 1) * t])
                for h in range(nr)]
        o_ref[0] = _assemble_heads(vals, lane)


def _attn_sample_b(cache_t, layer, page_table, q, kv_new, bias, o_cmp, o_win, gates, pps):
    bx, n_pages = page_table.shape
    page = cache_t.shape[3]
    t = q.shape[1]
    rows = N_Q_HEADS * t
    n = pps * page
    page_specs = [pl.BlockSpec((1, 1, 2 * KV_W, page),
                               lambda b, s, pt, k=k: (layer, pt[b, s * pps + k], 1, 0))
                  for k in range(pps)]
    return pl.pallas_call(
        functools.partial(_attn_sample_b_kernel, pps=pps),
        grid_spec=pltpu.PrefetchScalarGridSpec(
            num_scalar_prefetch=1,
            grid=(bx, n_pages // pps),
            in_specs=page_specs + [
                pl.BlockSpec((1, t, ATTN_W), lambda b, s, pt: (b, 0, 0)),
                pl.BlockSpec((1, t, 2 * KV_W), lambda b, s, pt: (b, 0, 1)),
                pl.BlockSpec((1, N_KV_HEADS, t, n), lambda b, s, pt: (b, 0, 0, s)),
                pl.BlockSpec((1, N_KV_HEADS, t, NEW_PAD), lambda b, s, pt: (b, 0, 0, n_pages * page // NEW_PAD)),
                pl.BlockSpec((1, rows, LANE), lambda b, s, pt: (b, 0, 0)),
                pl.BlockSpec((1, rows, LANE), lambda b, s, pt: (b, 0, 0)),
                pl.BlockSpec((1, t, GATE_PAD), lambda b, s, pt: (b, 0, 0))],
            out_specs=pl.BlockSpec((1, t, ATTN_W), lambda b, s, pt: (b, 0, 0)),
            scratch_shapes=[pltpu.VMEM((KV_W, n), BF16), pltpu.VMEM((KV_W, n), BF16),
                            pltpu.VMEM((N_Q_HEADS, t, 1), F32), pltpu.VMEM((N_Q_HEADS, t, 1), F32),
                            pltpu.VMEM((N_Q_HEADS, t, LANE), F32)]),
        out_shape=jax.ShapeDtypeStruct((bx, t, ATTN_W), F32),
        compiler_params=_params("parallel", "arbitrary"),
        name="attn_sample_b",
    )(page_table, *([cache_t] * pps), q, kv_new, bias, bias, o_cmp, o_win, gates)


def _mix_out_kernel(x_ref, mod_ref, oattn_ref, bg_ref, u_ref, uh_ref, p_ref, ph_ref, cw_ref, cb_ref,
                    pw_ref, ps_ref, wo_ref, norm_ref, x1_ref, h2_ref, *, pos0, zero_first_halo):
    bb, tt, d = x_ref.shape
    rows = bb * tt
    i = pl.program_id(1)
    uh = uh_ref[...]
    ph = ph_ref[...]
    if zero_first_halo:
        keep = jnp.where(i > 0, 1.0, 0.0)
        uh = uh * keep
        ph = ph * keep
    ucat = jnp.concatenate([uh, u_ref[...]], axis=1)
    conv = cb_ref[...]
    for j in range(CONV_K):
        off = CONV_HALO - CONV_BUF + j
        conv = conv + ucat[:, off:off + tt] * cw_ref[j:j + 1]
    y_conv = bg_ref[...] * conv
    pcat = jnp.concatenate([ph, p_ref[...]], axis=1)
    a2 = pcat[:, 1:] + pcat[:, :-1]
    a4 = a2[:, 2:] + a2[:, :-2]
    a8 = a4[:, 4:] + a4[:, :-4]
    a16 = a8[:, 8:] + a8[:, :-8]
    sums = (a2[:, POOL_HALO - 1:POOL_HALO - 1 + tt], a4[:, POOL_HALO - 3:POOL_HALO - 3 + tt],
            a8[:, POOL_HALO - 7:POOL_HALO - 7 + tt], a16[:, POOL_HALO - 15:POOL_HALO - 15 + tt])
    pos = pos0 + i * tt + _iota((1, tt, 1), 1)
    grp = _iota((1, 1, POOL_W), 2) // (POOL_W // len(POOL_WINDOWS))
    mean = jnp.zeros((bb, tt, POOL_W), F32)
    for gi, w in enumerate(POOL_WINDOWS):
        cnt = jnp.minimum(w, pos + 1).astype(F32)
        mean = jnp.where(grp == gi, sums[gi] / cnt, mean)
    dlt = (mean - p_ref[...]).reshape(rows, POOL_W).astype(BF16)
    y_pool = _dot(dlt, pw_ref[...]) * ps_ref[...]
    mix = _dot(oattn_ref[...].reshape(rows, ATTN_W).astype(BF16), wo_ref[0:ATTN_W])
    mix = mix + _dot(y_conv.reshape(rows, CONV_W).astype(BF16), wo_ref[ATTN_W:ATTN_W + CONV_W])
    mix = mix + _dot(y_pool.astype(BF16), wo_ref[ATTN_W + CONV_W:ATTN_W + CONV_W + POOL_W])
    gate1 = mod_ref[:, :, 2 * d:3 * d]
    x1 = x_ref[...] + gate1 * mix.reshape(bb, tt, d)
    x1_ref[...] = x1
    y = x1 * lax.rsqrt(jnp.mean(x1 * x1, axis=-1, keepdims=True) + EPS) * norm_ref[...]
    h2_ref[...] = (y * (1.0 + mod_ref[:, :, 4 * d:5 * d]) + mod_ref[:, :, 3 * d:4 * d]).astype(h2_ref.dtype)


def _mix_out(x, mod, oattn, bg, u, u_halo, pool, p_halo, conv_w, conv_b, pool_wbd, pool_scale, w_out, norm2,
             bb, tt, pos0, halo_from_self):
    bx, t, d = x.shape
    blk = lambda w: pl.BlockSpec((bb, tt, w), lambda b, i: (b, i, 0))
    full = lambda a: pl.BlockSpec(a.shape, lambda b, i: (0,) * a.ndim)
    if halo_from_self:
        uh_spec = pl.BlockSpec((bb, CONV_HALO, CONV_W),
                               lambda b, i: (b, jnp.maximum(i * (tt // CONV_HALO) - 1, 0), 0))
        ph_spec = pl.BlockSpec((bb, POOL_HALO, POOL_W),
                               lambda b, i: (b, jnp.maximum(i * (tt // POOL_HALO) - 1, 0), 0))
    else:
        uh_spec = pl.BlockSpec((bb, CONV_HALO, CONV_W), lambda b, i: (b, 0, 0))
        ph_spec = pl.BlockSpec((bb, POOL_HALO, POOL_W), lambda b, i: (b, 0, 0))
    return pl.pallas_call(
        functools.partial(_mix_out_kernel, pos0=pos0, zero_first_halo=halo_from_self),
        grid=(bx // bb, t // tt),
        in_specs=[blk(d), pl.BlockSpec((bb, 1, mod.shape[-1]), lambda b, i: (b, 0, 0)),
                  blk(ATTN_W), blk(CONV_W), blk(CONV_W), uh_spec, blk(POOL_W), ph_spec,
                  full(conv_w), full(conv_b), full(pool_wbd), full(pool_scale), full(w_out), full(norm2)],
        out_specs=[blk(d), blk(d)],
        out_shape=[jax.ShapeDtypeStruct((bx, t, d), F32), jax.ShapeDtypeStruct((bx, t, d), oattn.dtype)],
        compiler_params=_params("parallel", "parallel"),
        name="mix_out",
    )(x, mod, oattn, bg, u, u_halo, pool, p_halo, conv_w, conv_b, pool_wbd, pool_scale, w_out, norm2)


def _ffn_kernel(x1_ref, h2_ref, mod_ref, wa_ref, wb_ref, wd_ref, o_ref, acc_ref):
    bb, tt, d = x1_ref.shape
    j = pl.program_id(2)

    @pl.when(j == 0)
    def _():
        acc_ref[...] = jnp.zeros(acc_ref.shape, F32)

    h = h2_ref[...].reshape(bb * tt, d).astype(BF16)
    a = _dot(h, wa_ref[...])
    b = _dot(h, wb_ref[...])
    acc_ref[...] += _dot((_silu(a) * b).astype(BF16), wd_ref[...])

    @pl.when(j == pl.num_programs(2) - 1)
    def _():
        o_ref[...] = x1_ref[...] + mod_ref[:, :, 5 * d:6 * d] * acc_ref[...].reshape(bb, tt, d)


def _ffn(x1, h2, mod, w_up, w_down, bb, tt, th):
    bx, t, d = x1.shape
    hid = w_down.shape[0]
    nh = hid // th
    blk = pl.BlockSpec((bb, tt, d), lambda b, i, j: (b, i, 0))
    return pl.pallas_call(
        _ffn_kernel,
        grid=(bx // bb, t // tt, nh),
        in_specs=[blk, blk, pl.BlockSpec((bb, 1, mod.shape[-1]), lambda b, i, j: (b, 0, 0)),
                  pl.BlockSpec((d, th), lambda b, i, j: (0, j)),
                  pl.BlockSpec((d, th), lambda b, i, j: (0, nh + j)),
                  pl.BlockSpec((th, d), lambda b, i, j: (j, 0))],
        out_specs=blk,
        out_shape=jax.ShapeDtypeStruct((bx, t, d), F32),
        scratch_shapes=[pltpu.VMEM((bb * tt, d), F32)],
        compiler_params=_params("parallel", "parallel", "arbitrary"),
        name="ffn",
    )(x1, h2, mod, w_up, w_up, w_down)


def _overlap_matrix(n_cmp, n_sel, rows, lanes):
    m = np.zeros((rows, lanes), np.float32)
    i = np.arange(n_cmp)
    for part in range(CMP_BLOCK // CMP_STRIDE):
        j = np.minimum((i + part) * CMP_STRIDE // SLC_BLOCK, n_sel - 1)
        np.add.at(m, (i, j), 1.0)
    return m


def _expand_matrix(n_blocks, n_keys):
    return (np.arange(n_blocks)[:, None] == np.arange(n_keys)[None, :] // SLC_BLOCK).astype(np.float32)


def _chunk_order_perm(page):
    pos = np.arange(page)
    m = np.zeros((page, page), np.float32)
    m[pos, (pos % CMP_STRIDE) * (page // CMP_STRIDE) + pos // CMP_STRIDE] = 1.0
    return jnp.asarray(m, BF16)


def _seg_matrix(n):
    idx = np.arange(n) // HEAD_DIM
    return jnp.asarray((idx[:, None] == idx[None, :]).astype(np.float32) / HEAD_DIM, BF16)


def _prep_weights(w_in, w_out, q_norm, k_norm, cmp_pe, cmp_w1, cmp_w2, pool_w, w_up, w_down):
    depth, d, _ = w_in.shape
    g0 = ATTN_W + 6 * KV_W
    w_in_r = jnp.concatenate([w_in[:, :, :g0], w_in[:, :, g0 + N_Q_HEADS * N_BRANCH:],
                              w_in[:, :, g0:g0 + N_Q_HEADS * N_BRANCH],
                              jnp.zeros((depth, d, GATE_PAD - N_Q_HEADS * N_BRANCH), w_in.dtype)], axis=-1)
    w1r = cmp_w1.reshape(depth, 2, 2, CMP_STRIDE, HEAD_DIM, HEAD_DIM)
    wr = jnp.zeros((depth, 2, CMP_STRIDE, KV_W, 2 * KV_W), F32)
    w2bd = jnp.zeros((depth, 2, KV_W, KV_W), F32)
    for s in range(2):
        for g in range(N_KV_HEADS):
            r0 = g * HEAD_DIM
            w2bd = w2bd.at[:, s, r0:r0 + HEAD_DIM, r0:r0 + HEAD_DIM].set(cmp_w2[:, s])
            for half in range(2):
                c0 = half * KV_W + r0
                wr = wr.at[:, s, :, r0:r0 + HEAD_DIM, c0:c0 + HEAD_DIM].set(w1r[:, s, half])
    wr = wr.reshape(depth, 2, CMP_STRIDE * KV_W, 2 * KV_W)
    pe = cmp_pe.reshape(depth, 2, 2, CMP_STRIDE, 1, HEAD_DIM)
    pe = jnp.broadcast_to(pe, (depth, 2, 2, CMP_STRIDE, N_KV_HEADS, HEAD_DIM))
    pe = pe.reshape(depth, 2, 2, CMP_STRIDE * KV_W)
    pe_aug = jnp.concatenate([pe, jnp.zeros((depth, 2, 6, CMP_STRIDE * KV_W), F32)], axis=2)
    gw = POOL_W // len(POOL_WINDOWS)
    pool_bd = jnp.zeros((depth, POOL_W, POOL_W), F32)
    for g in range(len(POOL_WINDOWS)):
        pool_bd = pool_bd.at[:, g * gw:(g + 1) * gw, g * gw:(g + 1) * gw].set(pool_w[:, g])
    return dict(
        w_in=w_in_r.astype(BF16), w_out=w_out.astype(BF16), w_up=w_up.astype(BF16), w_down=w_down.astype(BF16),
        wr=wr.astype(BF16), pe_aug=pe_aug, w2bd=w2bd.astype(BF16), pool_bd=pool_bd.astype(BF16),
        q_gain=jnp.tile(q_norm, (1, N_Q_HEADS))[:, None, :],
        k_gain=jnp.tile(k_norm, (1, 1, N_KV_HEADS)),
    )


def _feature_major_view(a):
    lead = a.shape[:-2]
    f, p = a.shape[-2:]
    a = a.reshape(lead + (f // KV_W, N_KV_HEADS, HEAD_DIM, p))
    n = len(lead)
    return jnp.transpose(a, tuple(range(n)) + (n + 3, n, n + 1, n + 2))


def _layer_prompt(x, mod, l, W, P, C):
    bx, t, d = x.shape
    bb, tt = 1, ROW_TILE
    kg = P["k_gain"][l]
    q, bg, u, pool, gates, kv_t, win_t, raw, kst, vs, kwt, vw = _proj_in(
        x, mod, W["norm_mix"][l][None, None], P["w_in"][l], P["q_gain"][l], kg[1:3],
        C["seg_q"], C["seg_k"], bb, tt, True)
    kc, vc = _cmp_prompt(raw, P["wr"][l], P["pe_aug"][l], P["w2bd"][l], C["seg_k"], kg[0:1])
    oattn = _attn_prompt(q, kc, vc, kst, vs, kwt, vw, gates, C["ovt_prompt"], C["expand_prompt"])
    x1, h2 = _mix_out(x, mod, oattn, bg, u, u, pool, pool, W["conv_w"][l], W["conv_bias"][l][None],
                      P["pool_bd"][l], W["pool_scale"][l][None], P["w_out"][l], W["norm_ffn"][l][None, None],
                      bb, tt, 0, True)
    x2 = _ffn(x1, h2, mod, P["w_up"][l], P["w_down"][l], bb, tt, C["th"])
    states = (kv_t, win_t[:, :, t - min(WINDOW, t):], u[:, t - CONV_BUF:], pool[:, t - POOL_BUF:])
    return x2, states


def _layer_sample(x, mod, l, W, P, C, cache_t, page_table, win_state_t, conv_state, pool_state):
    bx, t, d = x.shape
    n_pages = page_table.shape[1]
    pos0 = n_pages * cache_t.shape[3]
    pps = C["pps"]
    kg = P["k_gain"][l]
    q, bg, u, pool, gates, kv, win = _proj_in(
        x, mod, W["norm_mix"][l][None, None], P["w_in"][l], P["q_gain"][l], kg[1:3],
        C["seg_q"], C["seg_k"], bx, t, False)
    kc, vc = _cmp_sample(cache_t, l, page_table, C["perm"], P["wr"][l], P["pe_aug"][l], P["w2bd"][l],
                         C["seg_k"], kg[0:1], pps)
    o_cmp, o_win, bias = _attn_sample_a(q, kc, vc, win_state_t, l, win, C["ov_sample"], C["expand_sample"],
                                        pos0, C["n_cmp_s"], C["n_sel_s"])
    oattn = _attn_sample_b(cache_t, l, page_table, q, kv, bias, o_cmp, o_win, gates, pps)
    u_halo = jnp.concatenate([jnp.zeros((bx, CONV_HALO - CONV_BUF, CONV_W), F32), conv_state], axis=1)
    p_halo = jnp.concatenate([jnp.zeros((bx, POOL_HALO - POOL_BUF, POOL_W), F32), pool_state], axis=1)
    x1, h2 = _mix_out(x, mod, oattn, bg, u, u_halo, pool, p_halo, W["conv_w"][l], W["conv_bias"][l][None],
                      P["pool_bd"][l], W["pool_scale"][l][None], P["w_out"][l], W["norm_ffn"][l][None, None],
                      bx, t, pos0, False)
    x2 = _ffn(x1, h2, mod, P["w_up"][l], P["w_down"][l], bx, t, C["th"])
    win_t = jnp.concatenate([win_state_t[l][:, :, t:], jnp.swapaxes(win, 1, 2)], axis=2)
    states = (kv.reshape(bx, t, 4, N_KV_HEADS, HEAD_DIM), win_t,
              jnp.concatenate([conv_state, u], axis=1)[:, t:],
              jnp.concatenate([pool_state, pool], axis=1)[:, t:])
    return x2, states


def kernel(x_prompt, x_sample, cache_nsa_kv, state_win_kv, state_conv, state_pool, page_table, c_prompt, c_sample, norm_mix, norm_ffn, w_ada, b_ada, w_in, w_out, q_norm, k_norm, cmp_pe, cmp_w1, cmp_w2, conv_w, conv_bias, pool_w, pool_scale, w_up, w_down):
    depth = w_in.shape[0]
    bp, tp, d = x_prompt.shape
    bs, ts, _ = x_sample.shape
    n_pages = page_table.shape[1]
    n_pool, page = cache_nsa_kv.shape[1:3]
    past = n_pages * page
    n_state = state_win_kv.shape[2]
    hid = w_down.shape[1]
    assert tp % ROW_TILE == 0 and tp >= WINDOW + TQ and past % SLC_BLOCK == 0 and ts <= CONV_HALO

    W = dict(norm_mix=norm_mix, norm_ffn=norm_ffn, conv_w=conv_w, conv_bias=conv_bias, pool_scale=pool_scale)
    P = _prep_weights(w_in, w_out, q_norm, k_norm, cmp_pe, cmp_w1, cmp_w2, pool_w, w_up, w_down)
    n_cmp_p = (tp - CMP_BLOCK) // CMP_STRIDE + 1
    n_sel_p = tp // SLC_BLOCK
    n_cmp_s = (past + ts - CMP_BLOCK) // CMP_STRIDE + 1
    n_sel_s = -(-(past + ts) // SLC_BLOCK)
    sel_lanes = -(-n_sel_s // LANE) * LANE
    pps = min(PAGES_PER_STEP, n_pages)
    C = dict(
        seg_q=_seg_matrix(ATTN_W), seg_k=_seg_matrix(KV_W), perm=_chunk_order_perm(page),
        ovt_prompt=jnp.asarray(_overlap_matrix(n_cmp_p, n_sel_p, tp // CMP_STRIDE, n_sel_p).T, BF16),
        expand_prompt=jnp.asarray(
            _expand_matrix(LANE, tp).reshape(LANE, tp // SLC_CHUNK, SLC_CHUNK).transpose(1, 0, 2), BF16),
        ov_sample=jnp.asarray(_overlap_matrix(n_cmp_s, n_sel_s, past // CMP_STRIDE, sel_lanes), BF16),
        expand_sample=jnp.asarray(_expand_matrix(sel_lanes, past + NEW_PAD), BF16),
        n_cmp_s=n_cmp_s, n_sel_s=n_sel_s, pps=pps,
        th=hid // 2 if (hid // 2) % LANE == 0 else hid,
    )
    cache_t = jnp.transpose(cache_nsa_kv, (0, 1, 3, 4, 5, 2)).reshape(depth, n_pool, 4 * KV_W, page)
    win_state_t = jnp.transpose(state_win_kv, (0, 1, 3, 4, 5, 2)).reshape(depth, bs, 2 * KV_W, n_state)
    mod = _ada(jnp.concatenate([c_prompt, c_sample], axis=0), w_ada, b_ada)
    mod_p = mod[:, :bp, None, :]
    mod_s = mod[:, bp:, None, :]

    yp, ys = x_prompt, x_sample
    outs_p, outs_s = [], []
    for l in range(depth):
        yp, st = _layer_prompt(yp, mod_p[l], l, W, P, C)
        outs_p.append(st)
        ys, st = _layer_sample(ys, mod_s[l], l, W, P, C, cache_t, page_table, win_state_t,
                               state_conv[l], state_pool[l])
        outs_s.append(st)
    stack = lambda outs, k: jnp.stack([o[k] for o in outs])
    return (yp, ys, _feature_major_view(stack(outs_p, 0)), stack(outs_s, 0),
            _feature_major_view(stack(outs_p, 1)), _feature_major_view(stack(outs_s, 1)),
            stack(outs_p, 2), stack(outs_s, 2), stack(outs_p, 3), stack(outs_s, 3))
```

```python
import functools

import numpy as np
import jax
import jax.numpy as jnp
from jax import lax
from jax.experimental import pallas as pl
from jax.experimental.pallas import tpu as pltpu

F32 = jnp.float32
BF16 = jnp.bfloat16

HEAD_DIM = 64
N_Q_HEADS = 8
N_KV_HEADS = 2
Q_PER_KV = N_Q_HEADS // N_KV_HEADS
N_BRANCH = 3
ATTN_W = N_Q_HEADS * HEAD_DIM
KV_W = N_KV_HEADS * HEAD_DIM
CONV_W = 256
POOL_W = 256
CONV_K = 3
CONV_BUF = CONV_K - 1
POOL_WINDOWS = (2, 4, 8, 16)
POOL_BUF = max(POOL_WINDOWS) - 1
CMP_BLOCK = 32
CMP_STRIDE = 16
SLC_BLOCK = 64
N_SLC = 16
WINDOW = 512
EPS = 1e-6
NEG = -1e30
FORCE = 1e4
ATTN_SCALE = HEAD_DIM ** -0.5
LOG2E = 1.4426950408889634
LANE = 128
GATE_PAD = LANE
IN_W_PAD = ATTN_W + 4 * KV_W + 2 * KV_W + 3 * CONV_W + POOL_W + GATE_PAD
CONV_HALO = 8
POOL_HALO = 16
NEW_PAD = LANE
VMEM_LIMIT = 56 * 1024 * 1024
ROW_TILE = 512
TQ = 128
SLC_CHUNK = 512
PAGES_PER_STEP = 16
ADA_TN = 1536

_NT = (((1,), (1,)), ((), ()))


def _params(*sem):
    return pltpu.CompilerParams(dimension_semantics=sem, vmem_limit_bytes=VMEM_LIMIT)


def _dot(a, b):
    return jnp.dot(a, b, preferred_element_type=F32)


def _dot_nt(a, b):
    return lax.dot_general(a, b, _NT, preferred_element_type=F32)


def _split_hilo(x):
    hi = x.astype(BF16)
    return hi, (x - hi.astype(F32)).astype(BF16)


def _dot_hilo(x, w_bf):
    hi, lo = _split_hilo(x)
    return _dot(hi, w_bf) + _dot(lo, w_bf)


def _seg_rms(x, seg_bf, gain):
    ms = _dot_hilo(x * x, seg_bf)
    return x * lax.rsqrt(ms + EPS) * gain


def _silu(x):
    return x * jax.nn.sigmoid(x)


def _pad_rows(x, n):
    return jnp.concatenate([x, jnp.zeros((n - x.shape[0], x.shape[1]), x.dtype)], axis=0)


def _iota(shape, axis):
    return lax.broadcasted_iota(jnp.int32, shape, axis)


def _ada_kernel(c_ref, w_ref, b_ref, o_ref):
    c = c_ref[...]
    o_ref[0] = _dot(_silu(c).astype(BF16), w_ref[0].astype(BF16)) + b_ref[0]


def _ada(c_all, w_ada, b_ada):
    depth, d, n = w_ada.shape
    r = c_all.shape[0]
    return pl.pallas_call(
        _ada_kernel,
        grid=(depth, n // ADA_TN),
        in_specs=[pl.BlockSpec((r, d), lambda l, j: (0, 0)),
                  pl.BlockSpec((1, d, ADA_TN), lambda l, j: (l, 0, j)),
                  pl.BlockSpec((1, 1, ADA_TN), lambda l, j: (l, 0, j))],
        out_specs=pl.BlockSpec((1, r, ADA_TN), lambda l, j: (l, 0, j)),
        out_shape=jax.ShapeDtypeStruct((depth, r, n), F32),
        compiler_params=_params("parallel", "parallel"),
        name="ada",
    )(c_all, w_ada, b_ada.reshape(depth, 1, n))


def _proj_in_kernel(x_ref, mod_ref, norm_ref, w_ref, qg_ref, kg_ref, seg_q_ref, seg_k_ref,
                    q_ref, bg_ref, u_ref, pool_ref, gates_ref, *kv_refs, feature_major):
    bb, tt, d = x_ref.shape
    rows = bb * tt
    x = x_ref[...]
    y = x * lax.rsqrt(jnp.mean(x * x, axis=-1, keepdims=True) + EPS) * norm_ref[...]
    shift1 = mod_ref[:, :, 0:d]
    scale1 = mod_ref[:, :, d:2 * d]
    h = (y * (1.0 + scale1) + shift1).reshape(rows, d).astype(BF16)
    z = _dot(h, w_ref[0])

    qn = _seg_rms(z[:, 0:ATTN_W], seg_q_ref[...], qg_ref[...]) * (ATTN_SCALE * LOG2E)
    q_ref[...] = qn.reshape(bb, tt, ATTN_W)
    o = ATTN_W
    raw_cmp = z[:, o:o + 2 * KV_W]
    ks = _seg_rms(z[:, o + 2 * KV_W:o + 3 * KV_W], seg_k_ref[...], kg_ref[0:1])
    vs = z[:, o + 3 * KV_W:o + 4 * KV_W]
    kv = jnp.concatenate([raw_cmp, ks, vs], axis=-1)
    o += 4 * KV_W
    kw = _seg_rms(z[:, o:o + KV_W], seg_k_ref[...], kg_ref[1:2])
    vw = z[:, o + KV_W:o + 2 * KV_W]
    win = jnp.concatenate([kw, vw], axis=-1)
    o += 2 * KV_W
    if feature_major:
        kvt_ref, wint_ref, raw_ref, ksa_ref, vst_ref, kw_ref, vwt_ref = kv_refs
        kvt = kv.T
        wint = win.T
        kvt_ref[0] = kvt
        wint_ref[0] = wint
        raw_ref[0] = raw_cmp
        key_block = (pl.program_id(1) * rows + _iota((rows, LANE), 0)) // SLC_BLOCK
        onehot = jnp.where(key_block == _iota((rows, LANE), 1), 1.0, 0.0)
        ksa_ref[0] = jnp.concatenate([ks, onehot], axis=-1).astype(BF16)
        vst = kvt[3 * KV_W:4 * KV_W].astype(BF16)
        for j in range(rows // SLC_CHUNK):
            vst_ref[0, j] = vst[:, j * SLC_CHUNK:(j + 1) * SLC_CHUNK]
        kw_ref[0] = kw.astype(BF16)
        vwt = wint[KV_W:2 * KV_W].astype(BF16)
        for j in range(rows // LANE):
            vwt_ref[0, j] = vwt[:, j * LANE:(j + 1) * LANE]
    else:
        kv_ref, win_ref = kv_refs
        kv_ref[...] = kv.reshape(bb, tt, 4 * KV_W)
        win_ref[...] = win.reshape(bb, tt, 2 * KV_W)
    bg_ref[...] = z[:, o:o + CONV_W].reshape(bb, tt, CONV_W)
    u_ref[...] = (z[:, o + CONV_W:o + 2 * CONV_W] * z[:, o + 2 * CONV_W:o + 3 * CONV_W]).reshape(bb, tt, CONV_W)
    o += 3 * CONV_W
    pool_ref[...] = z[:, o:o + POOL_W].reshape(bb, tt, POOL_W)
    o += POOL_W
    gates_ref[...] = jax.nn.sigmoid(z[:, o:o + GATE_PAD]).reshape(bb, tt, GATE_PAD)


def _proj_in(x, mod, norm1, w_in, layer, q_gain, k_gain2, seg_q, seg_k, bb, tt, feature_major):
    bx, t, d = x.shape
    w_spec = pl.BlockSpec((1,) + w_in.shape[1:], lambda b, i: (layer, 0, 0), pipeline_mode=pl.Buffered(1))
    grid = (bx // bb, t // tt)
    blk = lambda w: pl.BlockSpec((bb, tt, w), lambda b, i: (b, i, 0))
    full = lambda a: pl.BlockSpec(a.shape, lambda b, i: (0,) * a.ndim)
    outs = [((bx, t, ATTN_W), F32), ((bx, t, CONV_W), F32), ((bx, t, CONV_W), F32), ((bx, t, POOL_W), F32),
            ((bx, t, GATE_PAD), F32)]
    out_specs = [blk(s[-1]) for s, _ in outs]
    if feature_major:
        assert bb == 1 and tt % SLC_CHUNK == 0 and SLC_CHUNK % LANE == 0
        outs += [((bx, 4 * KV_W, t), F32), ((bx, 2 * KV_W, t), F32), ((bx, t, 2 * KV_W), F32),
                 ((bx, t, 2 * KV_W), BF16), ((bx, t // SLC_CHUNK, KV_W, SLC_CHUNK), BF16),
                 ((bx, t, KV_W), BF16), ((bx, t // LANE, KV_W, LANE), BF16)]
        out_specs += [pl.BlockSpec((1, 4 * KV_W, tt), lambda b, i: (b, 0, i)),
                      pl.BlockSpec((1, 2 * KV_W, tt), lambda b, i: (b, 0, i)),
                      blk(2 * KV_W),
                      blk(2 * KV_W),
                      pl.BlockSpec((1, tt // SLC_CHUNK, KV_W, SLC_CHUNK), lambda b, i: (b, i, 0, 0)),
                      blk(KV_W),
                      pl.BlockSpec((1, tt // LANE, KV_W, LANE), lambda b, i: (b, i, 0, 0))]
    else:
        outs += [((bx, t, 4 * KV_W), F32), ((bx, t, 2 * KV_W), F32)]
        out_specs += [blk(4 * KV_W), blk(2 * KV_W)]
    return pl.pallas_call(
        functools.partial(_proj_in_kernel, feature_major=feature_major),
        grid=grid,
        in_specs=[blk(d), pl.BlockSpec((bb, 1, mod.shape[-1]), lambda b, i: (b, 0, 0)),
                  full(norm1), w_spec, full(q_gain), full(k_gain2), full(seg_q), full(seg_k)],
        out_specs=out_specs,
        out_shape=[jax.ShapeDtypeStruct(s, dt) for s, dt in outs],
        compiler_params=_params("parallel", "parallel"),
        name="proj_in",
    )(x, mod, norm1, w_in, q_gain, k_gain2, seg_q, seg_k)


def _compress_core(chunks, nc, wr_ref, pe_ref, w2_ref, seg_k_ref, kg0_ref):
    outs = []
    for s in range(2):
        acc = _dot(chunks(s).astype(BF16), wr_ref[s])
        acc_pe = _dot(pe_ref[s].astype(BF16), wr_ref[s])
        bias = acc_pe[0:1, 0:KV_W] + acc_pe[1:2, KV_W:2 * KV_W]
        nxt = pltpu.roll(acc[:, KV_W:2 * KV_W], nc - 1, 0)
        pre = acc[:, 0:KV_W] + nxt + bias
        outs.append(_dot(_silu(pre).astype(BF16), w2_ref[s]))
    return _seg_rms(outs[0], seg_k_ref[...], kg0_ref[...]), outs[1]


def _cmp_prompt_kernel(k_ref, v_ref, wr_ref, pe_ref, w2_ref, seg_k_ref, kg0_ref, kc_ref, vc_ref):
    nc = kc_ref.shape[1]
    src = (k_ref, v_ref)
    read = lambda s: jnp.concatenate(
        [src[s][0, pl.ds(r, nc, stride=CMP_STRIDE), :] for r in range(CMP_STRIDE)], axis=1)
    kc, vc = _compress_core(read, nc, wr_ref, pe_ref, w2_ref, seg_k_ref, kg0_ref)
    kc_ref[0] = kc.astype(BF16)
    vc_ref[0] = vc.T.astype(BF16)


def _cmp_prompt(raw, wr, pe_aug, w2bd, seg_k, kg0):
    bx, t, _ = raw.shape
    nc = t // CMP_STRIDE
    full = lambda a: pl.BlockSpec(a.shape, lambda b: (0,) * a.ndim)
    return pl.pallas_call(
        _cmp_prompt_kernel,
        grid=(bx,),
        in_specs=[pl.BlockSpec((1, t, KV_W), lambda b: (b, 0, 0)),
                  pl.BlockSpec((1, t, KV_W), lambda b: (b, 0, 1)),
                  full(wr), full(pe_aug), full(w2bd), full(seg_k), full(kg0)],
        out_specs=[pl.BlockSpec((1, nc, KV_W), lambda b: (b, 0, 0)), pl.BlockSpec((1, KV_W, nc), lambda b: (b, 0, 0))],
        out_shape=[jax.ShapeDtypeStruct((bx, nc, KV_W), BF16), jax.ShapeDtypeStruct((bx, KV_W, nc), BF16)],
        compiler_params=_params("parallel"),
        name="cmp_prompt",
    )(raw, raw, wr, pe_aug, w2bd, seg_k, kg0)


def _cmp_sample_kernel(pt_ref, *refs, pps):
    pages = refs[:pps]
    perm_ref, wr_ref, pe_ref, w2_ref, seg_k_ref, kg0_ref, kc_ref, vc_ref, xs_ref = refs[pps:]
    step = pl.program_id(1)
    cpp = pages[0].shape[3] // CMP_STRIDE
    for k in range(0, pps, 2):
        row0 = pl.multiple_of((step * pps + k) * cpp, 2 * cpp)
        for s in range(2):
            zt = [_dot_nt(perm_ref[...], pages[k + j][0, 0, s * KV_W:(s + 1) * KV_W, :].astype(BF16))
                  for j in range(2)]
            for r in range(CMP_STRIDE):
                pair = jnp.concatenate([zt[0][r * cpp:(r + 1) * cpp], zt[1][r * cpp:(r + 1) * cpp]], axis=0)
                xs_ref[s, pl.ds(row0, 2 * cpp), r * KV_W:(r + 1) * KV_W] = pair.astype(BF16)

    @pl.when(step == pl.num_programs(1) - 1)
    def _():
        nc = kc_ref.shape[1]
        read = lambda s: xs_ref[s]
        kc, vc = _compress_core(read, nc, wr_ref, pe_ref, w2_ref, seg_k_ref, kg0_ref)
        kc_ref[0] = kc.astype(BF16)
        vc_ref[0] = vc.astype(BF16)


def _cmp_sample(cache_t, layer, page_table, perm, wr, pe_aug, w2bd, seg_k, kg0, pps):
    bx, n_pages = page_table.shape
    page = cache_t.shape[3]
    past = n_pages * page
    nc = past // CMP_STRIDE
    full = lambda a: pl.BlockSpec(a.shape, lambda b, s, pt: (0,) * a.ndim)
    page_specs = [pl.BlockSpec((1, 1, 2 * KV_W, page),
                               lambda b, s, pt, k=k: (layer, pt[b, s * pps + k], 0, 0))
                  for k in range(pps)]
    return pl.pallas_call(
        functools.partial(_cmp_sample_kernel, pps=pps),
        grid_spec=pltpu.PrefetchScalarGridSpec(
            num_scalar_prefetch=1,
            grid=(bx, n_pages // pps),
            in_specs=page_specs + [full(perm), full(wr), full(pe_aug), full(w2bd), full(seg_k), full(kg0)],
            out_specs=[pl.BlockSpec((1, nc, KV_W), lambda b, s, pt: (b, 0, 0))] * 2,
            scratch_shapes=[pltpu.VMEM((2, nc, CMP_STRIDE * KV_W), BF16)]),
        out_shape=[jax.ShapeDtypeStruct((bx, nc, KV_W), BF16)] * 2,
        compiler_params=_params("parallel", "arbitrary"),
        name="cmp_sample",
    )(page_table, *([cache_t] * pps), perm, wr, pe_aug, w2bd, seg_k, kg0)


def _q_rows(q, lane):
    out = []
    for g in range(N_KV_HEADS):
        keep = (lane // HEAD_DIM) == g
        rows = [jnp.where(keep, q[:, r * LANE:(r + 1) * LANE], 0.0) for r in range(Q_PER_KV)]
        out.append(jnp.concatenate(rows, axis=0).astype(BF16))
    return out


def _assemble_heads(vals, lane):
    return jnp.concatenate([jnp.where(lane < HEAD_DIM, vals[r], vals[Q_PER_KV + r]) for r in range(Q_PER_KV)],
                           axis=-1)


def _gate_combine(gates, h, o_cmp, o_slc, o_win):
    c = h * N_BRANCH
    return gates[:, c:c + 1] * o_cmp + gates[:, c + 1:c + 2] * o_slc + gates[:, c + 2:c + 3] * o_win


def _softmax_parts(s3):
    m = jnp.max(s3, axis=-1, keepdims=True)
    e = jnp.exp2(s3 - m)
    return e, jnp.sum(e, axis=-1, keepdims=True)


def _online_update(m, l, acc, s3, pv):
    m_new = jnp.maximum(m, jnp.max(s3, axis=-1, keepdims=True))
    alpha = jnp.exp2(m - m_new)
    e = jnp.exp2(s3 - m_new)
    return m_new, alpha * l + jnp.sum(e, axis=-1, keepdims=True), alpha * acc + pv(e)


def _select_rows(imp, cur, n_sel):
    jidx = _iota(imp.shape, 1)
    valid = jidx <= cur
    forced = valid & ((jidx == 0) | (jidx == cur) | (jidx == cur - 1))
    sc = jnp.where(forced, FORCE, jnp.where(valid, imp, NEG))
    rank = jnp.zeros(imp.shape, jnp.int32)
    for i in range(n_sel):
        col = sc[:, i:i + 1]
        beats = (col > sc) | ((col == sc) & (jidx > i))
        rank = rank + jnp.where(beats, 1, 0)
    return (rank < min(N_SLC, n_sel)) & valid


def _select_cols(imp_t, cur, n_sel):
    jidx = _iota(imp_t.shape, 0)
    valid = jidx <= cur
    forced = valid & ((jidx == 0) | (jidx == cur) | (jidx == cur - 1))
    sc = jnp.where(forced, FORCE, jnp.where(valid, imp_t, NEG))
    rank = jnp.zeros(imp_t.shape, jnp.int32)
    for i in range(n_sel):
        row = sc[i:i + 1, :]
        beats = (row > sc) | ((row == sc) & (jidx > i))
        rank = rank + jnp.where(beats, 1, 0)
    return jnp.where((rank < min(N_SLC, n_sel)) & valid, 1.0, 0.0)


def _attn_prompt_kernel(q_ref, kc_ref, vct_ref, ksa_ref, vst_ref, kw_ref, vwt_ref, gates_ref, ovt_ref, o_ref):
    tq = q_ref.shape[1]
    nc = kc_ref.shape[1]
    chunk = vst_ref.shape[3]
    n_sel = ovt_ref.shape[0]
    cols = Q_PER_KV * tq
    groups = range(N_KV_HEADS)
    t0 = pl.program_id(1) * tq
    lane = _iota((tq, LANE), 1)
    qpos = t0 + _iota((1, tq), 1)
    per_head = lambda a: jnp.concatenate([a] * Q_PER_KV, axis=1)
    q = q_ref[0]
    qt = []
    for g in groups:
        keep = (lane // HEAD_DIM) == g
        rows = jnp.concatenate([jnp.where(keep, q[:, r * LANE:(r + 1) * LANE], 0.0) for r in range(Q_PER_KV)],
                               axis=0)
        qt.append(rows.T.astype(BF16))
    row_lo = _iota((LANE, cols), 0) < HEAD_DIM
    row_half = _iota((LANE, 1), 0) // HEAD_DIM
    with_ones = lambda vt, g: jnp.where(row_half == g, vt, 1.0)

    def pair(a, normalised):
        o = jnp.where(row_lo, a[0], a[1])
        if normalised:
            return o
        return o / jnp.concatenate([a[0][HEAD_DIM:], a[1][:HEAD_DIM]], axis=0)

    cbias = per_head(jnp.where(_iota((nc, tq), 0) * CMP_STRIDE + CMP_BLOCK - 1 <= qpos, 0.0, NEG))
    cvalid = per_head(jnp.where(qpos >= CMP_BLOCK - 1, 1.0, 0.0))
    cur = qpos // SLC_BLOCK
    o_cmp, rhs = [], []
    for g in groups:
        s = _dot(kc_ref[0], qt[g]) + cbias
        e = jnp.exp2(s - jnp.max(s, axis=0, keepdims=True))
        p = e * (cvalid / jnp.sum(e, axis=0, keepdims=True))
        o_cmp.append(_dot(vct_ref[0], p.astype(BF16)))
        psum = p[:, 0:tq]
        for r in range(1, Q_PER_KV):
            psum = psum + p[:, r * tq:(r + 1) * tq]
        hi, lo = _split_hilo(psum)
        sel = _select_cols(_dot(ovt_ref[...], hi) + _dot(ovt_ref[...], lo), cur, n_sel)
        sel = jnp.concatenate([sel, jnp.zeros((LANE - n_sel, tq), F32)], axis=0)
        selneg = per_head(jnp.where(sel > 0.5, 0.0, NEG)).astype(BF16)
        rhs.append(jnp.concatenate([qt[g], selneg], axis=0))

    def slc_step(c, carry, causal):
        ka = ksa_ref[0, pl.ds(pl.multiple_of(c * chunk, chunk), chunk), :]
        vt = vst_ref[0, c]
        out = []
        for g in groups:
            m, acc = carry[g]
            s = _dot(ka, rhs[g])
            if causal is not None:
                s = jnp.where(causal, s, NEG)
            m_new = jnp.maximum(m, jnp.max(s, axis=0, keepdims=True))
            w = jnp.exp2(s - m_new).astype(BF16)
            out.append((m_new, jnp.exp2(m - m_new) * acc + _dot(with_ones(vt, g), w)))
        return tuple(out)

    init = tuple((jnp.full((1, cols), NEG, F32), jnp.zeros((LANE, cols), F32)) for g in groups)
    c_diag = t0 // chunk
    slc = lax.fori_loop(0, c_diag, lambda c, carry: slc_step(c, carry, None), init)
    slc = slc_step(c_diag, slc, per_head(c_diag * chunk + _iota((chunk, tq), 0) <= qpos))

    wtile0 = jnp.maximum(t0 - WINDOW, 0) // LANE
    wstart = pl.multiple_of(wtile0 * LANE, LANE)
    wn = WINDOW + tq
    wdiff = qpos - (wstart + _iota((wn, tq), 0))
    wbias = per_head(jnp.where((wdiff >= 0) & (wdiff < WINDOW), 0.0, NEG))
    kwin = kw_ref[0, pl.ds(wstart, wn), :]
    vwt = jnp.concatenate([vwt_ref[0, wtile0 + j] for j in range(wn // LANE)], axis=1)
    win = []
    for g in groups:
        s = _dot(kwin, qt[g]) + wbias
        w = jnp.exp2(s - jnp.max(s, axis=0, keepdims=True)).astype(BF16)
        win.append(_dot(with_ones(vwt, g), w))

    branches = (pair(o_cmp, True), pair([slc[g][1] for g in groups], False), pair(win, False))
    gt = gates_ref[0].T
    out = None
    for br in range(N_BRANCH):
        gate = jnp.concatenate(
            [jnp.concatenate([jnp.broadcast_to(gt[(g * Q_PER_KV + r) * N_BRANCH + br][None], (HEAD_DIM, tq))
                              for g in groups], axis=0) for r in range(Q_PER_KV)], axis=1)
        out = gate * branches[br] if out is None else out + gate * branches[br]
    out = out.T
    o_ref[0] = jnp.concatenate([out[r * tq:(r + 1) * tq] for r in range(Q_PER_KV)], axis=-1).astype(BF16)


def _attn_prompt(q, kc, vct, ksa, vst, kw, vwt, gates, ovt):
    bx, t, _ = q.shape
    whole = lambda a: pl.BlockSpec((1,) + a.shape[1:], lambda b, i: (b,) + (0,) * (a.ndim - 1))
    const = lambda a: pl.BlockSpec(a.shape, lambda b, i: (0,) * a.ndim)
    return pl.pallas_call(
        _attn_prompt_kernel,
        grid=(bx, t // TQ),
        in_specs=[pl.BlockSpec((1, TQ, ATTN_W), lambda b, i: (b, i, 0)),
                  whole(kc), whole(vct), whole(ksa), whole(vst), whole(kw), whole(vwt),
                  pl.BlockSpec((1, TQ, GATE_PAD), lambda b, i: (b, i, 0)),
                  const(ovt)],
        out_specs=pl.BlockSpec((1, TQ, ATTN_W), lambda b, i: (b, i, 0)),
        out_shape=jax.ShapeDtypeStruct((bx, t, ATTN_W), BF16),
        compiler_params=_params("parallel", "parallel"),
        name="attn_prompt",
    )(q, kc, vct, ksa, vst, kw, vwt, gates, ovt)


def _attn_sample_a_kernel(q_ref, kc_ref, vc_ref, wst_ref, wnew_ref, ov_ref, e_ref, ocmp_ref, owin_ref, bias_ref,
                          *, pos0, n_cmp, n_sel):
    t = q_ref.shape[1]
    nc = kc_ref.shape[1]
    n_state = wst_ref.shape[3]
    nr = N_Q_HEADS
    lane = _iota((t, LANE), 1)
    tcol = _iota((t, 1), 0)
    qpos = pos0 + tcol
    qbd = jnp.concatenate(_q_rows(q_ref[0], lane), axis=0)
    cidx = _iota((t, nc), 1)
    cbias = jnp.where((cidx < n_cmp) & ((cidx * CMP_STRIDE + CMP_BLOCK - 1) <= qpos), 0.0, NEG)
    cvalid = jnp.where(qpos >= CMP_BLOCK - 1, 1.0, 0.0)
    e, l = _softmax_parts(_dot_nt(qbd, kc_ref[0]).reshape(nr, t, nc) + cbias[None])
    p = e * (cvalid / l)
    ocmp_ref[0] = _dot(p.reshape(nr * t, nc).astype(BF16), vc_ref[0])
    p4 = p.reshape(N_KV_HEADS, Q_PER_KV, t, nc)
    psum = (p4[:, 0] + p4[:, 1] + p4[:, 2] + p4[:, 3]).reshape(N_KV_HEADS * t, nc)
    imp = _dot_hilo(psum, ov_ref[...])
    cur = jnp.concatenate([qpos // SLC_BLOCK] * N_KV_HEADS, axis=0)
    selneg = jnp.where(_select_rows(imp, cur, n_sel), 0.0, NEG).astype(BF16)
    bias_ref[0] = _dot(selneg, e_ref[...]).reshape(N_KV_HEADS, t, e_ref.shape[1])
    wst = wst_ref[0, 0]
    kwt = wst[0:KV_W].astype(BF16)
    vwt = wst[KV_W:2 * KV_W].astype(BF16)
    wnew = _pad_rows(wnew_ref[0], NEW_PAD).astype(BF16)
    sdiff = qpos - (pos0 - n_state + _iota((t, n_state), 1))
    sbias = jnp.where((sdiff >= 0) & (sdiff < WINDOW), 0.0, NEG)
    nbias = jnp.where(tcol - _iota((t, NEW_PAD), 1) >= 0, 0.0, NEG)
    s_a = _dot(qbd, kwt).reshape(nr, t, n_state) + sbias[None]
    s_b = _dot_nt(qbd, wnew[:, 0:KV_W]).reshape(nr, t, NEW_PAD) + nbias[None]
    m = jnp.maximum(jnp.max(s_a, axis=-1, keepdims=True), jnp.max(s_b, axis=-1, keepdims=True))
    e_a = jnp.exp2(s_a - m)
    e_b = jnp.exp2(s_b - m)
    l = jnp.sum(e_a, axis=-1, keepdims=True) + jnp.sum(e_b, axis=-1, keepdims=True)
    o = (_dot_nt(e_a.reshape(nr * t, n_state).astype(BF16), vwt)
         + _dot(e_b.reshape(nr * t, NEW_PAD).astype(BF16), wnew[:, KV_W:2 * KV_W]))
    owin_ref[0] = o / l.reshape(nr * t, 1)


def _attn_sample_a(q, kc, vc, win_state_t, layer, win_new, ov, expand, pos0, n_cmp, n_sel):
    bx, t, _ = q.shape
    nc = kc.shape[1]
    n_state = win_state_t.shape[3]
    kp = expand.shape[1]
    rows = N_Q_HEADS * t
    per_b = lambda a: pl.BlockSpec((1,) + a.shape[1:], lambda b: (b,) + (0,) * (a.ndim - 1))
    const = lambda a: pl.BlockSpec(a.shape, lambda b: (0,) * a.ndim)
    return pl.pallas_call(
        functools.partial(_attn_sample_a_kernel, pos0=pos0, n_cmp=n_cmp, n_sel=n_sel),
        grid=(bx,),
        in_specs=[per_b(q), per_b(kc), per_b(vc),
                  pl.BlockSpec((1, 1, 2 * KV_W, n_state), lambda b: (layer, b, 0, 0)),
                  per_b(win_new), const(ov), const(expand)],
        out_specs=[pl.BlockSpec((1, rows, LANE), lambda b: (b, 0, 0)),
                   pl.BlockSpec((1, rows, LANE), lambda b: (b, 0, 0)),
                   pl.BlockSpec((1, N_KV_HEADS, t, kp), lambda b: (b, 0, 0, 0))],
        out_shape=[jax.ShapeDtypeStruct((bx, rows, LANE), F32),
                   jax.ShapeDtypeStruct((bx, rows, LANE), F32),
                   jax.ShapeDtypeStruct((bx, N_KV_HEADS, t, kp), F32)],
        compiler_params=_params("parallel"),
        name="attn_sample_a",
    )(q, kc, vc, win_state_t, win_new, ov, expand)


def _attn_sample_b_kernel(pt_ref, *refs, pps):
    pages = refs[:pps]
    (q_ref, kvn_ref, bias_ref, ocmp_ref, owin_ref, gates_ref, o_ref, s_ref, vt_ref) = refs[pps:]
    t = q_ref.shape[1]
    page = pages[0].shape[3]
    n = pps * page
    nr = N_Q_HEADS
    n_steps = s_ref.shape[0]
    step = pl.program_id(1)
    lane = _iota((t, LANE), 1)
    qbd = jnp.concatenate(_q_rows(q_ref[0], lane), axis=0)
    kt = jnp.concatenate([pages[k][0, 0, 0:KV_W, :].astype(BF16) for k in range(pps)], axis=1)
    s_ref[step] = _dot(qbd, kt)
    for k in range(pps):
        vt_ref[step, :, k * page:(k + 1) * page] = pages[k][0, 0, KV_W:2 * KV_W, :].astype(BF16)

    @pl.when(step == n_steps - 1)
    def _():
        biased = lambda s_flat, bias: (s_flat.reshape(N_KV_HEADS, Q_PER_KV, t, s_flat.shape[-1])
                                       + bias[:, None]).reshape(nr, t, s_flat.shape[-1])
        parts = [biased(s_ref[j], bias_ref[0, :, :, j * n:(j + 1) * n]) for j in range(n_steps)]
        kvn = _pad_rows(kvn_ref[0], NEW_PAD).astype(BF16)
        causal = _iota((t, NEW_PAD), 1) <= _iota((t, 1), 0)
        bias_new = jnp.where(causal[None], bias_ref[0, :, :, n_steps * n:n_steps * n + NEW_PAD], NEG)
        s_new = biased(_dot_nt(qbd, kvn[:, 0:KV_W]), bias_new)
        m = jnp.max(s_new, axis=-1, keepdims=True)
        for p in parts:
            m = jnp.maximum(m, jnp.max(p, axis=-1, keepdims=True))
        e = jnp.exp2(s_new - m)
        l = jnp.sum(e, axis=-1, keepdims=True)
        acc = _dot(e.reshape(nr * t, NEW_PAD).astype(BF16), kvn[:, KV_W:2 * KV_W])
        for j, p in enumerate(parts):
            e = jnp.exp2(p - m)
            l = l + jnp.sum(e, axis=-1, keepdims=True)
            acc = acc + _dot_nt(e.reshape(nr * t, n).astype(BF16), vt_ref[j])
        o_slc = acc.reshape(nr, t, LANE) / l
        gates = gates_ref[0]
        vals = [_gate_combine(gates, h, ocmp_ref[0, h * t:(h + 1) * t], o_slc[h], owin_ref[0, h * t:(h + 1) * t])
                for h in range(nr)]
        o_ref[0] = _assemble_heads(vals, lane)


def _attn_sample_b(cache_t, layer, page_table, q, kv_new, bias, o_cmp, o_win, gates, pps):
    bx, n_pages = page_table.shape
    page = cache_t.shape[3]
    t = q.shape[1]
    rows = N_Q_HEADS * t
    n = pps * page
    page_specs = [pl.BlockSpec((1, 1, 2 * KV_W, page),
                               lambda b, s, pt, k=k: (layer, pt[b, s * pps + k], 1, 0))
                  for k in range(pps)]
    return pl.pallas_call(
        functools.partial(_attn_sample_b_kernel, pps=pps),
        grid_spec=pltpu.PrefetchScalarGridSpec(
            num_scalar_prefetch=1,
            grid=(bx, n_pages // pps),
            in_specs=page_specs + [
                pl.BlockSpec((1, t, ATTN_W), lambda b, s, pt: (b, 0, 0)),
                pl.BlockSpec((1, t, 2 * KV_W), lambda b, s, pt: (b, 0, 1)),
                pl.BlockSpec((1,) + bias.shape[1:], lambda b, s, pt: (b, 0, 0, 0)),
                pl.BlockSpec((1, rows, LANE), lambda b, s, pt: (b, 0, 0)),
                pl.BlockSpec((1, rows, LANE), lambda b, s, pt: (b, 0, 0)),
                pl.BlockSpec((1, t, GATE_PAD), lambda b, s, pt: (b, 0, 0))],
            out_specs=pl.BlockSpec((1, t, ATTN_W), lambda b, s, pt: (b, 0, 0)),
            scratch_shapes=[pltpu.VMEM((n_pages // pps, rows, n), F32),
                            pltpu.VMEM((n_pages // pps, KV_W, n), BF16)]),
        out_shape=jax.ShapeDtypeStruct((bx, t, ATTN_W), F32),
        compiler_params=_params("parallel", "arbitrary"),
        name="attn_sample_b",
    )(page_table, *([cache_t] * pps), q, kv_new, bias, o_cmp, o_win, gates)


def _mix_out_kernel(x_ref, mod_ref, oattn_ref, bg_ref, u_ref, uh_ref, p_ref, ph_ref, cw_ref, cb_ref,
                    pw_ref, ps_ref, wo_ref, norm_ref, x1_ref, h2_ref, *, pos0, zero_first_halo):
    bb, tt, d = x_ref.shape
    rows = bb * tt
    i = pl.program_id(1)
    uh = uh_ref[...]
    ph = ph_ref[...]
    if zero_first_halo:
        keep = jnp.where(i > 0, 1.0, 0.0)
        uh = uh * keep
        ph = ph * keep
    ucat = jnp.concatenate([uh, u_ref[...]], axis=1)
    conv = cb_ref[...]
    for j in range(CONV_K):
        off = CONV_HALO - CONV_BUF + j
        conv = conv + ucat[:, off:off + tt] * cw_ref[j:j + 1]
    y_conv = bg_ref[...] * conv
    pcat = jnp.concatenate([ph, p_ref[...]], axis=1)
    a2 = pcat[:, 1:] + pcat[:, :-1]
    a4 = a2[:, 2:] + a2[:, :-2]
    a8 = a4[:, 4:] + a4[:, :-4]
    a16 = a8[:, 8:] + a8[:, :-8]
    sums = (a2[:, POOL_HALO - 1:POOL_HALO - 1 + tt], a4[:, POOL_HALO - 3:POOL_HALO - 3 + tt],
            a8[:, POOL_HALO - 7:POOL_HALO - 7 + tt], a16[:, POOL_HALO - 15:POOL_HALO - 15 + tt])
    pos = pos0 + i * tt + _iota((1, tt, 1), 1)
    grp = _iota((1, 1, POOL_W), 2) // (POOL_W // len(POOL_WINDOWS))
    mean = jnp.zeros((bb, tt, POOL_W), F32)
    for gi, w in enumerate(POOL_WINDOWS):
        cnt = jnp.minimum(w, pos + 1).astype(F32)
        mean = jnp.where(grp == gi, sums[gi] / cnt, mean)
    dlt = (mean - p_ref[...]).reshape(rows, POOL_W).astype(BF16)
    y_pool = _dot(dlt, pw_ref[...]) * ps_ref[...]
    mix = _dot(oattn_ref[...].reshape(rows, ATTN_W).astype(BF16), wo_ref[0, 0:ATTN_W])
    mix = mix + _dot(y_conv.reshape(rows, CONV_W).astype(BF16), wo_ref[0, ATTN_W:ATTN_W + CONV_W])
    mix = mix + _dot(y_pool.astype(BF16), wo_ref[0, ATTN_W + CONV_W:ATTN_W + CONV_W + POOL_W])
    gate1 = mod_ref[:, :, 2 * d:3 * d]
    x1 = x_ref[...] + gate1 * mix.reshape(bb, tt, d)
    x1_ref[...] = x1
    y = x1 * lax.rsqrt(jnp.mean(x1 * x1, axis=-1, keepdims=True) + EPS) * norm_ref[...]
    h2_ref[...] = (y * (1.0 + mod_ref[:, :, 4 * d:5 * d]) + mod_ref[:, :, 3 * d:4 * d]).astype(h2_ref.dtype)


def _mix_out(x, mod, oattn, bg, u, u_halo, pool, p_halo, conv_w, conv_b, pool_wbd, pool_scale, w_out, layer, norm2,
             bb, tt, pos0, halo_from_self):
    bx, t, d = x.shape
    wo_spec = pl.BlockSpec((1,) + w_out.shape[1:], lambda b, i: (layer, 0, 0), pipeline_mode=pl.Buffered(1))
    blk = lambda w: pl.BlockSpec((bb, tt, w), lambda b, i: (b, i, 0))
    full = lambda a: pl.BlockSpec(a.shape, lambda b, i: (0,) * a.ndim)
    if halo_from_self:
        uh_spec = pl.BlockSpec((bb, CONV_HALO, CONV_W),
                               lambda b, i: (b, jnp.maximum(i * (tt // CONV_HALO) - 1, 0), 0))
        ph_spec = pl.BlockSpec((bb, POOL_HALO, POOL_W),
                               lambda b, i: (b, jnp.maximum(i * (tt // POOL_HALO) - 1, 0), 0))
    else:
        uh_spec = pl.BlockSpec((bb, CONV_HALO, CONV_W), lambda b, i: (b, 0, 0))
        ph_spec = pl.BlockSpec((bb, POOL_HALO, POOL_W), lambda b, i: (b, 0, 0))
    return pl.pallas_call(
        functools.partial(_mix_out_kernel, pos0=pos0, zero_first_halo=halo_from_self),
        grid=(bx // bb, t // tt),
        in_specs=[blk(d), pl.BlockSpec((bb, 1, mod.shape[-1]), lambda b, i: (b, 0, 0)),
                  blk(ATTN_W), blk(CONV_W), blk(CONV_W), uh_spec, blk(POOL_W), ph_spec,
                  full(conv_w), full(conv_b), full(pool_wbd), full(pool_scale), wo_spec, full(norm2)],
        out_specs=[blk(d), blk(d)],
        out_shape=[jax.ShapeDtypeStruct((bx, t, d), F32), jax.ShapeDtypeStruct((bx, t, d), oattn.dtype)],
        compiler_params=_params("parallel", "parallel"),
        name="mix_out",
    )(x, mod, oattn, bg, u, u_halo, pool, p_halo, conv_w, conv_b, pool_wbd, pool_scale, w_out, norm2)


def _ffn_kernel(x1_ref, h2_ref, mod_ref, wa_ref, wb_ref, wd_ref, o_ref):
    bb, tt, d = x1_ref.shape
    h = h2_ref[...].reshape(bb * tt, d).astype(BF16)
    a = _dot(h, wa_ref[0])
    b = _dot(h, wb_ref[0])
    y = _dot((_silu(a) * b).astype(BF16), wd_ref[0])
    o_ref[...] = x1_ref[...] + mod_ref[:, :, 5 * d:6 * d] * y.reshape(bb, tt, d)


def _ffn(x1, h2, mod, w_up, w_down, layer, bb, tt):
    bx, t, d = x1.shape
    hid = w_down.shape[1]
    blk = pl.BlockSpec((bb, tt, d), lambda b, i: (b, i, 0))
    resident = lambda shape, imap: pl.BlockSpec(shape, imap, pipeline_mode=pl.Buffered(1))
    return pl.pallas_call(
        _ffn_kernel,
        grid=(bx // bb, t // tt),
        in_specs=[blk, blk, pl.BlockSpec((bb, 1, mod.shape[-1]), lambda b, i: (b, 0, 0)),
                  resident((1, d, hid), lambda b, i: (layer, 0, 0)),
                  resident((1, d, hid), lambda b, i: (layer, 0, 1)),
                  resident((1, hid, d), lambda b, i: (layer, 0, 0))],
        out_specs=blk,
        out_shape=jax.ShapeDtypeStruct((bx, t, d), F32),
        compiler_params=_params("parallel", "parallel"),
        name="ffn",
    )(x1, h2, mod, w_up, w_up, w_down)


def _overlap_matrix(n_cmp, n_sel, rows, lanes):
    m = np.zeros((rows, lanes), np.float32)
    i = np.arange(n_cmp)
    for part in range(CMP_BLOCK // CMP_STRIDE):
        j = np.minimum((i + part) * CMP_STRIDE // SLC_BLOCK, n_sel - 1)
        np.add.at(m, (i, j), 1.0)
    return m


def _expand_matrix(n_blocks, n_keys):
    return (np.arange(n_blocks)[:, None] == np.arange(n_keys)[None, :] // SLC_BLOCK).astype(np.float32)


def _chunk_order_perm(page):
    pos = np.arange(page)
    m = np.zeros((page, page), np.float32)
    m[(pos % CMP_STRIDE) * (page // CMP_STRIDE) + pos // CMP_STRIDE, pos] = 1.0
    return jnp.asarray(m, BF16)


def _seg_matrix(n):
    idx = np.arange(n) // HEAD_DIM
    return jnp.asarray((idx[:, None] == idx[None, :]).astype(np.float32) / HEAD_DIM, BF16)


def _prep_weights(w_in, w_out, q_norm, k_norm, cmp_pe, cmp_w1, cmp_w2, pool_w, w_up, w_down):
    depth, d, _ = w_in.shape
    g0 = ATTN_W + 6 * KV_W
    pair_order = np.concatenate([np.arange(h * HEAD_DIM, (h + 1) * HEAD_DIM)
                                 for r in range(Q_PER_KV) for h in (r, Q_PER_KV + r)])
    w_in_r = jnp.concatenate([w_in[:, :, pair_order], w_in[:, :, ATTN_W:g0], w_in[:, :, g0 + N_Q_HEADS * N_BRANCH:],
                              w_in[:, :, g0:g0 + N_Q_HEADS * N_BRANCH],
                              jnp.zeros((depth, d, GATE_PAD - N_Q_HEADS * N_BRANCH), w_in.dtype)], axis=-1)
    w_out_r = jnp.concatenate([w_out[:, pair_order], w_out[:, ATTN_W:]], axis=1)
    eye_g = jnp.eye(N_KV_HEADS, dtype=F32)
    w1r = cmp_w1.reshape(depth, 2, 2, CMP_STRIDE, HEAD_DIM, HEAD_DIM)
    w1t = jnp.transpose(w1r, (0, 1, 3, 4, 2, 5))
    wr = w1t[:, :, :, None, :, :, None, :] * eye_g[None, None, None, :, None, None, :, None]
    wr = wr.reshape(depth, 2, CMP_STRIDE * KV_W, 2 * KV_W)
    w2bd = (cmp_w2[:, :, None, :, None, :] * eye_g[None, None, :, None, :, None]).reshape(depth, 2, KV_W, KV_W)
    pe = cmp_pe.reshape(depth, 2, 2, CMP_STRIDE, 1, HEAD_DIM)
    pe = jnp.broadcast_to(pe, (depth, 2, 2, CMP_STRIDE, N_KV_HEADS, HEAD_DIM))
    pe = pe.reshape(depth, 2, 2, CMP_STRIDE * KV_W)
    pe_aug = jnp.concatenate([pe, jnp.zeros((depth, 2, 6, CMP_STRIDE * KV_W), F32)], axis=2)
    eye_p = jnp.eye(len(POOL_WINDOWS), dtype=F32)
    pool_bd = (pool_w[:, :, :, None, :] * eye_p[None, :, None, :, None]).reshape(depth, POOL_W, POOL_W)
    return dict(
        w_in=w_in_r.astype(BF16), w_out=w_out_r.astype(BF16), w_up=w_up.astype(BF16), w_down=w_down.astype(BF16),
        wr=wr.astype(BF16), pe_aug=pe_aug, w2bd=w2bd.astype(BF16), pool_bd=pool_bd.astype(BF16),
        q_gain=jnp.tile(q_norm, (1, N_Q_HEADS))[:, None, :],
        k_gain=jnp.tile(k_norm, (1, 1, N_KV_HEADS)),
    )


def _feature_major_view(a):
    lead = a.shape[:-2]
    f, p = a.shape[-2:]
    a = a.reshape(lead + (f // KV_W, N_KV_HEADS, HEAD_DIM, p))
    n = len(lead)
    return jnp.transpose(a, tuple(range(n)) + (n + 3, n, n + 1, n + 2))


def _layer_prompt(x, mod, l, W, P, C):
    bx, t, d = x.shape
    bb, tt = 1, ROW_TILE
    kg = P["k_gain"][l]
    q, bg, u, pool, gates, kv_t, win_t, raw, ksa, vst, kw, vwt = _proj_in(
        x, mod, W["norm_mix"][l][None, None], P["w_in"], l, P["q_gain"][l], kg[1:3],
        C["seg_q"], C["seg_k"], bb, tt, True)
    kc, vct = _cmp_prompt(raw, P["wr"][l], P["pe_aug"][l], P["w2bd"][l], C["seg_k"], kg[0:1])
    oattn = _attn_prompt(q, kc, vct, ksa, vst, kw, vwt, gates, C["ovt_prompt"])
    x1, h2 = _mix_out(x, mod, oattn, bg, u, u, pool, pool, W["conv_w"][l], W["conv_bias"][l][None],
                      P["pool_bd"][l], W["pool_scale"][l][None], P["w_out"], l, W["norm_ffn"][l][None, None],
                      bb, tt, 0, True)
    x2 = _ffn(x1, h2, mod, P["w_up"], P["w_down"], l, bb, tt)
    states = (kv_t, win_t[:, :, t - min(WINDOW, t):], u[:, t - CONV_BUF:], pool[:, t - POOL_BUF:])
    return x2, states


def _layer_sample(x, mod, l, W, P, C, cache_t, page_table, win_state_t, conv_state, pool_state):
    bx, t, d = x.shape
    n_pages = page_table.shape[1]
    pos0 = n_pages * cache_t.shape[3]
    pps = C["pps"]
    kg = P["k_gain"][l]
    q, bg, u, pool, gates, kv, win = _proj_in(
        x, mod, W["norm_mix"][l][None, None], P["w_in"], l, P["q_gain"][l], kg[1:3],
        C["seg_q"], C["seg_k"], bx, t, False)
    kc, vc = _cmp_sample(cache_t, l, page_table, C["perm"], P["wr"][l], P["pe_aug"][l], P["w2bd"][l],
                         C["seg_k"], kg[0:1], pps)
    o_cmp, o_win, bias = _attn_sample_a(q, kc, vc, win_state_t, l, win, C["ov_sample"], C["expand_sample"],
                                        pos0, C["n_cmp_s"], C["n_sel_s"])
    oattn = _attn_sample_b(cache_t, l, page_table, q, kv, bias, o_cmp, o_win, gates, pps)
    u_halo = jnp.concatenate([jnp.zeros((bx, CONV_HALO - CONV_BUF, CONV_W), F32), conv_state], axis=1)
    p_halo = jnp.concatenate([jnp.zeros((bx, POOL_HALO - POOL_BUF, POOL_W), F32), pool_state], axis=1)
    x1, h2 = _mix_out(x, mod, oattn, bg, u, u_halo, pool, p_halo, W["conv_w"][l], W["conv_bias"][l][None],
                      P["pool_bd"][l], W["pool_scale"][l][None], P["w_out"], l, W["norm_ffn"][l][None, None],
                      bx, t, pos0, False)
    x2 = _ffn(x1, h2, mod, P["w_up"], P["w_down"], l, bx, t)
    win_t = jnp.concatenate([win_state_t[l][:, :, t:], jnp.swapaxes(win, 1, 2)], axis=2)
    states = (kv.reshape(bx, t, 4, N_KV_HEADS, HEAD_DIM), win_t,
              jnp.concatenate([conv_state, u], axis=1)[:, t:],
              jnp.concatenate([pool_state, pool], axis=1)[:, t:])
    return x2, states


def kernel(x_prompt, x_sample, cache_nsa_kv, state_win_kv, state_conv, state_pool, page_table, c_prompt, c_sample, norm_mix, norm_ffn, w_ada, b_ada, w_in, w_out, q_norm, k_norm, cmp_pe, cmp_w1, cmp_w2, conv_w, conv_bias, pool_w, pool_scale, w_up, w_down):
    depth = w_in.shape[0]
    bp, tp, d = x_prompt.shape
    bs, ts, _ = x_sample.shape
    n_pages = page_table.shape[1]
    n_pool, page = cache_nsa_kv.shape[1:3]
    past = n_pages * page
    n_state = state_win_kv.shape[2]
    hid = w_down.shape[1]
    assert tp % ROW_TILE == 0 and tp >= WINDOW + TQ and past % SLC_BLOCK == 0 and ts <= CONV_HALO

    W = dict(norm_mix=norm_mix, norm_ffn=norm_ffn, conv_w=conv_w, conv_bias=conv_bias, pool_scale=pool_scale)
    P = _prep_weights(w_in, w_out, q_norm, k_norm, cmp_pe, cmp_w1, cmp_w2, pool_w, w_up, w_down)
    n_cmp_p = (tp - CMP_BLOCK) // CMP_STRIDE + 1
    n_sel_p = tp // SLC_BLOCK
    n_cmp_s = (past + ts - CMP_BLOCK) // CMP_STRIDE + 1
    n_sel_s = -(-(past + ts) // SLC_BLOCK)
    sel_lanes = -(-n_sel_s // LANE) * LANE
    pps = min(PAGES_PER_STEP, n_pages)
    C = dict(
        seg_q=_seg_matrix(ATTN_W), seg_k=_seg_matrix(KV_W), perm=_chunk_order_perm(page),
        ovt_prompt=jnp.asarray(_overlap_matrix(n_cmp_p, n_sel_p, tp // CMP_STRIDE, n_sel_p).T, BF16),
        ov_sample=jnp.asarray(_overlap_matrix(n_cmp_s, n_sel_s, past // CMP_STRIDE, sel_lanes), BF16),
        expand_sample=jnp.asarray(_expand_matrix(sel_lanes, past + NEW_PAD), BF16),
        n_cmp_s=n_cmp_s, n_sel_s=n_sel_s, pps=pps,
    )
    cache_t = jnp.transpose(cache_nsa_kv, (0, 1, 3, 4, 5, 2)).reshape(depth, n_pool, 4 * KV_W, page)
    win_state_t = jnp.transpose(state_win_kv, (0, 1, 3, 4, 5, 2)).reshape(depth, bs, 2 * KV_W, n_state)
    mod = _ada(jnp.concatenate([c_prompt, c_sample], axis=0), w_ada, b_ada)
    mod_p = mod[:, :bp, None, :]
    mod_s = mod[:, bp:, None, :]

    yp, ys = x_prompt, x_sample
    outs_p, outs_s = [], []
    for l in range(depth):
        yp, st = _layer_prompt(yp, mod_p[l], l, W, P, C)
        outs_p.append(st)
        ys, st = _layer_sample(ys, mod_s[l], l, W, P, C, cache_t, page_table, win_state_t,
                               state_conv[l], state_pool[l])
        outs_s.append(st)
    stack = lambda outs, k: jnp.stack([o[k] for o in outs])
    return (yp, ys, _feature_major_view(stack(outs_p, 0)), stack(outs_s, 0),
            _feature_major_view(stack(outs_p, 1)), _feature_major_view(stack(outs_s, 1)),
            stack(outs_p, 2), stack(outs_s, 2), stack(outs_p, 3), stack(outs_s, 3))
```

```python
import functools

import numpy as np
import jax
import jax.numpy as jnp
from jax import lax
from jax.experimental import pallas as pl
from jax.experimental.pallas import tpu as pltpu

F32 = jnp.float32
BF16 = jnp.bfloat16

HEAD_DIM = 64
N_Q_HEADS = 8
N_KV_HEADS = 2
Q_PER_KV = N_Q_HEADS // N_KV_HEADS
N_BRANCH = 3
ATTN_W = N_Q_HEADS * HEAD_DIM
KV_W = N_KV_HEADS * HEAD_DIM
CONV_W = 256
POOL_W = 256
CONV_K = 3
CONV_BUF = CONV_K - 1
POOL_WINDOWS = (2, 4, 8, 16)
POOL_BUF = max(POOL_WINDOWS) - 1
CMP_BLOCK = 32
CMP_STRIDE = 16
SLC_BLOCK = 64
N_SLC = 16
WINDOW = 512
EPS = 1e-6
NEG = -1e30
FORCE = 1e4
ATTN_SCALE = HEAD_DIM ** -0.5
LOG2E = 1.4426950408889634
LANE = 128
GATE_PAD = LANE
IN_W_PAD = ATTN_W + 4 * KV_W + 2 * KV_W + 3 * CONV_W + POOL_W + GATE_PAD
CONV_HALO = 8
POOL_HALO = 16
NEW_PAD = LANE
VMEM_LIMIT = 56 * 1024 * 1024
ROW_TILE = 512
TQ = 512
SLC_CHUNK = 512
PAGES_PER_STEP = 16
ADA_TN = 1536

_NT = (((1,), (1,)), ((), ()))


def _params(*sem):
    return pltpu.CompilerParams(dimension_semantics=sem, vmem_limit_bytes=VMEM_LIMIT)


def _dot(a, b):
    return jnp.dot(a, b, preferred_element_type=F32)


def _dot_nt(a, b):
    return lax.dot_general(a, b, _NT, preferred_element_type=F32)


def _split_hilo(x):
    hi = x.astype(BF16)
    return hi, (x - hi.astype(F32)).astype(BF16)


def _dot_hilo(x, w_bf):
    hi, lo = _split_hilo(x)
    return _dot(hi, w_bf) + _dot(lo, w_bf)


def _seg_rms(x, seg_bf, gain):
    ms = _dot_hilo(x * x, seg_bf)
    return x * lax.rsqrt(ms + EPS) * gain


def _silu(x):
    return x * jax.nn.sigmoid(x)


def _pad_rows(x, n):
    return jnp.concatenate([x, jnp.zeros((n - x.shape[0], x.shape[1]), x.dtype)], axis=0)


def _iota(shape, axis):
    return lax.broadcasted_iota(jnp.int32, shape, axis)


def _ada_kernel(c_ref, w_ref, b_ref, o_ref):
    c = c_ref[...]
    o_ref[0] = _dot(_silu(c).astype(BF16), w_ref[0].astype(BF16)) + b_ref[0]


def _ada(c_all, w_ada, b_ada):
    depth, d, n = w_ada.shape
    r = c_all.shape[0]
    return pl.pallas_call(
        _ada_kernel,
        grid=(depth, n // ADA_TN),
        in_specs=[pl.BlockSpec((r, d), lambda l, j: (0, 0)),
                  pl.BlockSpec((1, d, ADA_TN), lambda l, j: (l, 0, j)),
                  pl.BlockSpec((1, 1, ADA_TN), lambda l, j: (l, 0, j))],
        out_specs=pl.BlockSpec((1, r, ADA_TN), lambda l, j: (l, 0, j)),
        out_shape=jax.ShapeDtypeStruct((depth, r, n), F32),
        compiler_params=_params("parallel", "parallel"),
        name="ada",
    )(c_all, w_ada, b_ada.reshape(depth, 1, n))


def _proj_in_kernel(x_ref, mod_ref, norm_ref, w_ref, qg_ref, kg_ref, seg_q_ref, seg_k_ref,
                    q_ref, bg_ref, u_ref, pool_ref, gates_ref, *kv_refs, feature_major):
    bb, tt, d = x_ref.shape
    rows = bb * tt
    x = x_ref[...]
    y = x * lax.rsqrt(jnp.mean(x * x, axis=-1, keepdims=True) + EPS) * norm_ref[...]
    shift1 = mod_ref[:, :, 0:d]
    scale1 = mod_ref[:, :, d:2 * d]
    h = (y * (1.0 + scale1) + shift1).reshape(rows, d).astype(BF16)
    z = _dot(h, w_ref[0])

    qn = _seg_rms(z[:, 0:ATTN_W], seg_q_ref[...], qg_ref[...]) * (ATTN_SCALE * LOG2E)
    q_ref[...] = qn.reshape(bb, tt, ATTN_W)
    o = ATTN_W
    raw_cmp = z[:, o:o + 2 * KV_W]
    ks = _seg_rms(z[:, o + 2 * KV_W:o + 3 * KV_W], seg_k_ref[...], kg_ref[0:1])
    vs = z[:, o + 3 * KV_W:o + 4 * KV_W]
    kv = jnp.concatenate([raw_cmp, ks, vs], axis=-1)
    o += 4 * KV_W
    kw = _seg_rms(z[:, o:o + KV_W], seg_k_ref[...], kg_ref[1:2])
    vw = z[:, o + KV_W:o + 2 * KV_W]
    win = jnp.concatenate([kw, vw], axis=-1)
    o += 2 * KV_W
    if feature_major:
        kvt_ref, wint_ref, raw_ref, ksa_ref, vst_ref, kw_ref, vwt_ref = kv_refs
        kvt = kv.T
        wint = win.T
        kvt_ref[0] = kvt
        wint_ref[0] = wint
        raw_ref[0] = raw_cmp
        key_block = (pl.program_id(1) * rows + _iota((rows, LANE), 0)) // SLC_BLOCK
        onehot = jnp.where(key_block == _iota((rows, LANE), 1), 1.0, 0.0)
        ksa_ref[0] = jnp.concatenate([ks, onehot], axis=-1).astype(BF16)
        vst = kvt[3 * KV_W:4 * KV_W].astype(BF16)
        for j in range(rows // SLC_CHUNK):
            vst_ref[0, j] = vst[:, j * SLC_CHUNK:(j + 1) * SLC_CHUNK]
        kw_ref[0] = kw.astype(BF16)
        vwt = wint[KV_W:2 * KV_W].astype(BF16)
        for j in range(rows // LANE):
            vwt_ref[0, j] = vwt[:, j * LANE:(j + 1) * LANE]
    else:
        kv_ref, win_ref = kv_refs
        kv_ref[...] = kv.reshape(bb, tt, 4 * KV_W)
        win_ref[...] = win.reshape(bb, tt, 2 * KV_W)
    bg_ref[...] = z[:, o:o + CONV_W].reshape(bb, tt, CONV_W)
    u_ref[...] = (z[:, o + CONV_W:o + 2 * CONV_W] * z[:, o + 2 * CONV_W:o + 3 * CONV_W]).reshape(bb, tt, CONV_W)
    o += 3 * CONV_W
    pool_ref[...] = z[:, o:o + POOL_W].reshape(bb, tt, POOL_W)
    o += POOL_W
    gates_ref[...] = jax.nn.sigmoid(z[:, o:o + GATE_PAD]).reshape(bb, tt, GATE_PAD)


def _proj_in(x, mod, norm1, w_in, layer, q_gain, k_gain2, seg_q, seg_k, bb, tt, feature_major):
    bx, t, d = x.shape
    w_spec = pl.BlockSpec((1,) + w_in.shape[1:], lambda b, i: (layer, 0, 0), pipeline_mode=pl.Buffered(1))
    grid = (bx // bb, t // tt)
    blk = lambda w: pl.BlockSpec((bb, tt, w), lambda b, i: (b, i, 0))
    full = lambda a: pl.BlockSpec(a.shape, lambda b, i: (0,) * a.ndim)
    outs = [((bx, t, ATTN_W), F32), ((bx, t, CONV_W), F32), ((bx, t, CONV_W), F32), ((bx, t, POOL_W), F32),
            ((bx, t, GATE_PAD), F32)]
    out_specs = [blk(s[-1]) for s, _ in outs]
    if feature_major:
        assert bb == 1 and tt % SLC_CHUNK == 0 and SLC_CHUNK % LANE == 0
        outs += [((bx, 4 * KV_W, t), F32), ((bx, 2 * KV_W, t), F32), ((bx, t, 2 * KV_W), F32),
                 ((bx, t, 2 * KV_W), BF16), ((bx, t // SLC_CHUNK, KV_W, SLC_CHUNK), BF16),
                 ((bx, t, KV_W), BF16), ((bx, t // LANE, KV_W, LANE), BF16)]
        out_specs += [pl.BlockSpec((1, 4 * KV_W, tt), lambda b, i: (b, 0, i)),
                      pl.BlockSpec((1, 2 * KV_W, tt), lambda b, i: (b, 0, i)),
                      blk(2 * KV_W),
                      blk(2 * KV_W),
                      pl.BlockSpec((1, tt // SLC_CHUNK, KV_W, SLC_CHUNK), lambda b, i: (b, i, 0, 0)),
                      blk(KV_W),
                      pl.BlockSpec((1, tt // LANE, KV_W, LANE), lambda b, i: (b, i, 0, 0))]
    else:
        outs += [((bx, t, 4 * KV_W), F32), ((bx, t, 2 * KV_W), F32)]
        out_specs += [blk(4 * KV_W), blk(2 * KV_W)]
    return pl.pallas_call(
        functools.partial(_proj_in_kernel, feature_major=feature_major),
        grid=grid,
        in_specs=[blk(d), pl.BlockSpec((bb, 1, mod.shape[-1]), lambda b, i: (b, 0, 0)),
                  full(norm1), w_spec, full(q_gain), full(k_gain2), full(seg_q), full(seg_k)],
        out_specs=out_specs,
        out_shape=[jax.ShapeDtypeStruct(s, dt) for s, dt in outs],
        compiler_params=_params("parallel", "parallel"),
        name="proj_in",
    )(x, mod, norm1, w_in, q_gain, k_gain2, seg_q, seg_k)


def _compress_core(chunks, nc, wr_ref, pe_ref, w2_ref, seg_k_ref, kg0_ref):
    accs = [_dot(chunks(s).astype(BF16), wr_ref[s]) for s in range(2)]
    outs = []
    for s in range(2):
        acc = accs[s]
        acc_pe = _dot(pe_ref[s].astype(BF16), wr_ref[s])
        bias = acc_pe[0:1, 0:KV_W] + acc_pe[1:2, KV_W:2 * KV_W]
        nxt = pltpu.roll(acc[:, KV_W:2 * KV_W], nc - 1, 0)
        pre = acc[:, 0:KV_W] + nxt + bias
        outs.append(_dot(_silu(pre).astype(BF16), w2_ref[s]))
    return _seg_rms(outs[0], seg_k_ref[...], kg0_ref[...]), outs[1]


def _cmp_prompt_kernel(k_ref, v_ref, wr_ref, pe_ref, w2_ref, seg_k_ref, kg0_ref, kc_ref, vc_ref):
    nc = kc_ref.shape[1]
    src = (k_ref, v_ref)
    read = lambda s: jnp.concatenate(
        [src[s][0, pl.ds(r, nc, stride=CMP_STRIDE), :] for r in range(CMP_STRIDE)], axis=1)
    kc, vc = _compress_core(read, nc, wr_ref, pe_ref, w2_ref, seg_k_ref, kg0_ref)
    kc_ref[0] = kc.astype(BF16)
    vc_ref[0] = vc.T.astype(BF16)


def _cmp_prompt(raw, wr, pe_aug, w2bd, seg_k, kg0):
    bx, t, _ = raw.shape
    nc = t // CMP_STRIDE
    full = lambda a: pl.BlockSpec(a.shape, lambda b: (0,) * a.ndim)
    return pl.pallas_call(
        _cmp_prompt_kernel,
        grid=(bx,),
        in_specs=[pl.BlockSpec((1, t, KV_W), lambda b: (b, 0, 0)),
                  pl.BlockSpec((1, t, KV_W), lambda b: (b, 0, 1)),
                  full(wr), full(pe_aug), full(w2bd), full(seg_k), full(kg0)],
        out_specs=[pl.BlockSpec((1, nc, KV_W), lambda b: (b, 0, 0)), pl.BlockSpec((1, KV_W, nc), lambda b: (b, 0, 0))],
        out_shape=[jax.ShapeDtypeStruct((bx, nc, KV_W), BF16), jax.ShapeDtypeStruct((bx, KV_W, nc), BF16)],
        compiler_params=_params("parallel"),
        name="cmp_prompt",
    )(raw, raw, wr, pe_aug, w2bd, seg_k, kg0)


def _cmp_sample_kernel(pt_ref, *refs, pps):
    pages = refs[:pps]
    perm_ref, wr_ref, pe_ref, w2_ref, seg_k_ref, kg0_ref, kc_ref, vc_ref, xs_ref = refs[pps:]
    step = pl.program_id(1)
    cpp = pages[0].shape[3] // CMP_STRIDE
    reorder = (lambda tile: _dot_nt(perm_ref[0], tile), lambda tile: _dot(tile, perm_ref[1]).T)
    for k in range(0, pps, 2):
        row0 = pl.multiple_of((step * pps + k) * cpp, 2 * cpp)
        for s in range(2):
            zt = [reorder[s](pages[k + j][0, 0, s * KV_W:(s + 1) * KV_W, :].astype(BF16))
                  for j in range(2)]
            for r in range(CMP_STRIDE):
                pair = jnp.concatenate([zt[0][r * cpp:(r + 1) * cpp], zt[1][r * cpp:(r + 1) * cpp]], axis=0)
                xs_ref[s, pl.ds(row0, 2 * cpp), r * KV_W:(r + 1) * KV_W] = pair.astype(BF16)

    @pl.when(step == pl.num_programs(1) - 1)
    def _():
        nc = kc_ref.shape[1]
        read = lambda s: xs_ref[s]
        kc, vc = _compress_core(read, nc, wr_ref, pe_ref, w2_ref, seg_k_ref, kg0_ref)
        kc_ref[0] = kc.astype(BF16)
        vc_ref[0] = vc.astype(BF16)


def _cmp_sample(cache_t, layer, page_table, perm, wr, pe_aug, w2bd, seg_k, kg0, pps):
    bx, n_pages = page_table.shape
    page = cache_t.shape[3]
    past = n_pages * page
    nc = past // CMP_STRIDE
    full = lambda a: pl.BlockSpec(a.shape, lambda b, s, pt: (0,) * a.ndim)
    page_specs = [pl.BlockSpec((1, 1, 2 * KV_W, page),
                               lambda b, s, pt, k=k: (layer, pt[b, s * pps + k], 0, 0))
                  for k in range(pps)]
    return pl.pallas_call(
        functools.partial(_cmp_sample_kernel, pps=pps),
        grid_spec=pltpu.PrefetchScalarGridSpec(
            num_scalar_prefetch=1,
            grid=(bx, n_pages // pps),
            in_specs=page_specs + [full(perm), full(wr), full(pe_aug), full(w2bd), full(seg_k), full(kg0)],
            out_specs=[pl.BlockSpec((1, nc, KV_W), lambda b, s, pt: (b, 0, 0))] * 2,
            scratch_shapes=[pltpu.VMEM((2, nc, CMP_STRIDE * KV_W), BF16)]),
        out_shape=[jax.ShapeDtypeStruct((bx, nc, KV_W), BF16)] * 2,
        compiler_params=_params("parallel", "arbitrary"),
        name="cmp_sample",
    )(page_table, *([cache_t] * pps), perm, wr, pe_aug, w2bd, seg_k, kg0)


def _q_rows(q, lane):
    out = []
    for g in range(N_KV_HEADS):
        keep = (lane // HEAD_DIM) == g
        rows = []
        for r in range(Q_PER_KV):
            h = g * Q_PER_KV + r
            blk = q[:, (h // 2) * LANE:(h // 2 + 1) * LANE]
            if h % 2 != g:
                blk = pltpu.roll(blk, HEAD_DIM, 1)
            rows.append(jnp.where(keep, blk, 0.0))
        out.append(jnp.concatenate(rows, axis=0).astype(BF16))
    return out


def _assemble_heads(vals, lane):
    cols = []
    for k in range(N_Q_HEADS // 2):
        a, b = vals[2 * k], vals[2 * k + 1]
        if (2 * k) // Q_PER_KV == 1:
            a = pltpu.roll(a, HEAD_DIM, 1)
        else:
            b = pltpu.roll(b, HEAD_DIM, 1)
        cols.append(jnp.where(lane < HEAD_DIM, a, b))
    return jnp.concatenate(cols, axis=-1)


def _gate_combine(gates, h, o_cmp, o_slc, o_win):
    c = h * N_BRANCH
    return gates[:, c:c + 1] * o_cmp + gates[:, c + 1:c + 2] * o_slc + gates[:, c + 2:c + 3] * o_win


def _softmax_parts(s3):
    m = jnp.max(s3, axis=-1, keepdims=True)
    e = jnp.exp2(s3 - m)
    return e, jnp.sum(e, axis=-1, keepdims=True)


def _online_update(m, l, acc, s3, pv):
    m_new = jnp.maximum(m, jnp.max(s3, axis=-1, keepdims=True))
    alpha = jnp.exp2(m - m_new)
    e = jnp.exp2(s3 - m_new)
    return m_new, alpha * l + jnp.sum(e, axis=-1, keepdims=True), alpha * acc + pv(e)


def _select_rows(imp, cur, n_sel):
    jidx = _iota(imp.shape, 1)
    valid = jidx <= cur
    forced = valid & ((jidx == 0) | (jidx == cur) | (jidx == cur - 1))
    sc = jnp.where(forced, FORCE, jnp.where(valid, imp, NEG))
    rank = jnp.zeros(imp.shape, jnp.int32)
    for i in range(n_sel):
        col = sc[:, i:i + 1]
        beats = (col > sc) | ((col == sc) & (jidx > i))
        rank = rank + jnp.where(beats, 1, 0)
    return (rank < min(N_SLC, n_sel)) & valid


def _select_cols(imp_t, cur, n_sel):
    jidx = _iota(imp_t.shape, 0)
    valid = jidx <= cur
    forced = valid & ((jidx == 0) | (jidx == cur) | (jidx == cur - 1))
    sc = jnp.where(forced, FORCE, jnp.where(valid, imp_t, NEG))
    rank = jnp.zeros(imp_t.shape, jnp.int32)
    for i in range(n_sel):
        row = sc[i:i + 1, :]
        beats = (row > sc) | ((row == sc) & (jidx > i))
        rank = rank + jnp.where(beats, 1, 0)
    return jnp.where((rank < min(N_SLC, n_sel)) & valid, 1.0, 0.0)


def _attn_prompt_kernel(q_ref, kc_ref, vct_ref, ksa_ref, vst_ref, kw_ref, vwt_ref, gates_ref, ovt_ref, o_ref):
    tq = q_ref.shape[1]
    nc = kc_ref.shape[1]
    chunk = vst_ref.shape[3]
    n_sel = ovt_ref.shape[0]
    cols = Q_PER_KV * tq
    groups = range(N_KV_HEADS)
    t0 = pl.program_id(1) * tq
    qpos = t0 + _iota((1, tq), 1)
    per_head = lambda a: jnp.concatenate([a] * Q_PER_KV, axis=1)
    q_t = q_ref[0].T
    zero = jnp.zeros((HEAD_DIM, tq), F32)
    qt = []
    for g in groups:
        blocks = []
        for r in range(Q_PER_KV):
            h = g * Q_PER_KV + r
            head = q_t[h * HEAD_DIM:(h + 1) * HEAD_DIM]
            blocks.append(jnp.concatenate([head, zero] if g == 0 else [zero, head], axis=0))
        qt.append(jnp.concatenate(blocks, axis=1).astype(BF16))
    row_lo = _iota((LANE, cols), 0) < HEAD_DIM
    row_half = _iota((LANE, 1), 0) // HEAD_DIM
    with_ones = lambda vt, g: jnp.where(row_half == g, vt, 1.0)

    def pair(a, normalised):
        o = jnp.where(row_lo, a[0], a[1])
        if normalised:
            return o
        return o / jnp.concatenate([a[0][HEAD_DIM:], a[1][:HEAD_DIM]], axis=0)

    cbias = per_head(jnp.where(_iota((nc, tq), 0) * CMP_STRIDE + CMP_BLOCK - 1 <= qpos, 0.0, NEG))
    cvalid = per_head(jnp.where(qpos >= CMP_BLOCK - 1, 1.0, 0.0))
    cur = qpos // SLC_BLOCK
    o_cmp, rhs = [], []
    for g in groups:
        s = _dot(kc_ref[0], qt[g]) + cbias
        e = jnp.exp2(s - jnp.max(s, axis=0, keepdims=True))
        p = e * (cvalid / jnp.sum(e, axis=0, keepdims=True))
        o_cmp.append(_dot(vct_ref[0], p.astype(BF16)))
        psum = p[:, 0:tq]
        for r in range(1, Q_PER_KV):
            psum = psum + p[:, r * tq:(r + 1) * tq]
        hi, lo = _split_hilo(psum)
        sel = _select_cols(_dot(ovt_ref[...], hi) + _dot(ovt_ref[...], lo), cur, n_sel)
        sel = jnp.concatenate([sel, jnp.zeros((LANE - n_sel, tq), F32)], axis=0)
        selneg = per_head(jnp.where(sel > 0.5, 0.0, NEG)).astype(BF16)
        rhs.append(jnp.concatenate([qt[g], selneg], axis=0))

    def slc_step(c, carry, causal):
        ka = ksa_ref[0, pl.ds(pl.multiple_of(c * chunk, chunk), chunk), :]
        vt = vst_ref[0, c]
        out = []
        for g in groups:
            m, acc = carry[g]
            s = _dot(ka, rhs[g])
            if causal is not None:
                s = jnp.where(causal, s, NEG)
            m_new = jnp.maximum(m, jnp.max(s, axis=0, keepdims=True))
            w = jnp.exp2(s - m_new).astype(BF16)
            out.append((m_new, jnp.exp2(m - m_new) * acc + _dot(with_ones(vt, g), w)))
        return tuple(out)

    init = tuple((jnp.full((1, cols), NEG, F32), jnp.zeros((LANE, cols), F32)) for g in groups)
    c_diag = t0 // chunk
    slc = lax.fori_loop(0, c_diag, lambda c, carry: slc_step(c, carry, None), init)
    slc = slc_step(c_diag, slc, per_head(c_diag * chunk + _iota((chunk, tq), 0) <= qpos))

    wtile0 = jnp.maximum(t0 - WINDOW, 0) // LANE
    wstart = pl.multiple_of(wtile0 * LANE, LANE)
    wn = WINDOW + tq
    wdiff = qpos - (wstart + _iota((wn, tq), 0))
    wbias = per_head(jnp.where((wdiff >= 0) & (wdiff < WINDOW), 0.0, NEG))
    kwin = kw_ref[0, pl.ds(wstart, wn), :]
    vwt = jnp.concatenate([vwt_ref[0, wtile0 + j] for j in range(wn // LANE)], axis=1)
    win = []
    for g in groups:
        s = _dot(kwin, qt[g]) + wbias
        w = jnp.exp2(s - jnp.max(s, axis=0, keepdims=True)).astype(BF16)
        win.append(_dot(with_ones(vwt, g), w))

    branches = (pair(o_cmp, True), pair([slc[g][1] for g in groups], False), pair(win, False))
    gt = gates_ref[0].T
    out = None
    for br in range(N_BRANCH):
        gate = jnp.concatenate(
            [jnp.concatenate([jnp.broadcast_to(gt[(g * Q_PER_KV + r) * N_BRANCH + br][None], (HEAD_DIM, tq))
                              for g in groups], axis=0) for r in range(Q_PER_KV)], axis=1)
        out = gate * branches[br] if out is None else out + gate * branches[br]
    heads = [out[g * HEAD_DIM:(g + 1) * HEAD_DIM, r * tq:(r + 1) * tq] for g in groups for r in range(Q_PER_KV)]
    o_ref[0] = jnp.concatenate(heads, axis=0).T.astype(BF16)


def _attn_prompt(q, kc, vct, ksa, vst, kw, vwt, gates, ovt):
    bx, t, _ = q.shape
    whole = lambda a: pl.BlockSpec((1,) + a.shape[1:], lambda b, i: (b,) + (0,) * (a.ndim - 1))
    const = lambda a: pl.BlockSpec(a.shape, lambda b, i: (0,) * a.ndim)
    return pl.pallas_call(
        _attn_prompt_kernel,
        grid=(bx, t // TQ),
        in_specs=[pl.BlockSpec((1, TQ, ATTN_W), lambda b, i: (b, i, 0)),
                  whole(kc), whole(vct), whole(ksa), whole(vst), whole(kw), whole(vwt),
                  pl.BlockSpec((1, TQ, GATE_PAD), lambda b, i: (b, i, 0)),
                  const(ovt)],
        out_specs=pl.BlockSpec((1, TQ, ATTN_W), lambda b, i: (b, i, 0)),
        out_shape=jax.ShapeDtypeStruct((bx, t, ATTN_W), BF16),
        compiler_params=_params("parallel", "parallel"),
        name="attn_prompt",
    )(q, kc, vct, ksa, vst, kw, vwt, gates, ovt)


def _attn_sample_a_kernel(q_ref, kc_ref, vc_ref, wst_ref, wnew_ref, ov_ref, e_ref, ocmp_ref, owin_ref, bias_ref,
                          *, pos0, n_cmp, n_sel):
    t = q_ref.shape[1]
    nc = kc_ref.shape[1]
    n_state = wst_ref.shape[3]
    nr = N_Q_HEADS
    lane = _iota((t, LANE), 1)
    tcol = _iota((t, 1), 0)
    qpos = pos0 + tcol
    qbd = jnp.concatenate(_q_rows(q_ref[0], lane), axis=0)
    cidx = _iota((t, nc), 1)
    cbias = jnp.where((cidx < n_cmp) & ((cidx * CMP_STRIDE + CMP_BLOCK - 1) <= qpos), 0.0, NEG)
    cvalid = jnp.where(qpos >= CMP_BLOCK - 1, 1.0, 0.0)
    e, l = _softmax_parts(_dot_nt(qbd, kc_ref[0]).reshape(nr, t, nc) + cbias[None])
    p = e * (cvalid / l)
    ocmp_ref[0] = _dot(p.reshape(nr * t, nc).astype(BF16), vc_ref[0])
    p4 = p.reshape(N_KV_HEADS, Q_PER_KV, t, nc)
    psum = (p4[:, 0] + p4[:, 1] + p4[:, 2] + p4[:, 3]).reshape(N_KV_HEADS * t, nc)
    imp = _dot_hilo(psum, ov_ref[...])
    cur = jnp.concatenate([qpos // SLC_BLOCK] * N_KV_HEADS, axis=0)
    selneg = jnp.where(_select_rows(imp, cur, n_sel), 0.0, NEG).astype(BF16)
    bias_ref[0] = _dot(selneg, e_ref[...]).reshape(N_KV_HEADS, t, e_ref.shape[1])
    wst = wst_ref[0, 0]
    kwt = wst[0:KV_W].astype(BF16)
    vwt = wst[KV_W:2 * KV_W].astype(BF16)
    wnew = _pad_rows(wnew_ref[0], NEW_PAD).astype(BF16)
    sdiff = qpos - (pos0 - n_state + _iota((t, n_state), 1))
    sbias = jnp.where((sdiff >= 0) & (sdiff < WINDOW), 0.0, NEG)
    nbias = jnp.where(tcol - _iota((t, NEW_PAD), 1) >= 0, 0.0, NEG)
    s_a = _dot(qbd, kwt).reshape(nr, t, n_state) + sbias[None]
    s_b = _dot_nt(qbd, wnew[:, 0:KV_W]).reshape(nr, t, NEW_PAD) + nbias[None]
    m = jnp.maximum(jnp.max(s_a, axis=-1, keepdims=True), jnp.max(s_b, axis=-1, keepdims=True))
    e_a = jnp.exp2(s_a - m)
    e_b = jnp.exp2(s_b - m)
    l = jnp.sum(e_a, axis=-1, keepdims=True) + jnp.sum(e_b, axis=-1, keepdims=True)
    o = (_dot_nt(e_a.reshape(nr * t, n_state).astype(BF16), vwt)
         + _dot(e_b.reshape(nr * t, NEW_PAD).astype(BF16), wnew[:, KV_W:2 * KV_W]))
    owin_ref[0] = o / l.reshape(nr * t, 1)


def _attn_sample_a(q, kc, vc, win_state_t, layer, win_new, ov, expand, pos0, n_cmp, n_sel):
    bx, t, _ = q.shape
    nc = kc.shape[1]
    n_state = win_state_t.shape[3]
    kp = expand.shape[1]
    rows = N_Q_HEADS * t
    per_b = lambda a: pl.BlockSpec((1,) + a.shape[1:], lambda b: (b,) + (0,) * (a.ndim - 1))
    const = lambda a: pl.BlockSpec(a.shape, lambda b: (0,) * a.ndim)
    return pl.pallas_call(
        functools.partial(_attn_sample_a_kernel, pos0=pos0, n_cmp=n_cmp, n_sel=n_sel),
        grid=(bx,),
        in_specs=[per_b(q), per_b(kc), per_b(vc),
                  pl.BlockSpec((1, 1, 2 * KV_W, n_state), lambda b: (layer, b, 0, 0)),
                  per_b(win_new), const(ov), const(expand)],
        out_specs=[pl.BlockSpec((1, rows, LANE), lambda b: (b, 0, 0)),
                   pl.BlockSpec((1, rows, LANE), lambda b: (b, 0, 0)),
                   pl.BlockSpec((1, N_KV_HEADS, t, kp), lambda b: (b, 0, 0, 0))],
        out_shape=[jax.ShapeDtypeStruct((bx, rows, LANE), F32),
                   jax.ShapeDtypeStruct((bx, rows, LANE), F32),
                   jax.ShapeDtypeStruct((bx, N_KV_HEADS, t, kp), F32)],
        compiler_params=_params("parallel"),
        name="attn_sample_a",
    )(q, kc, vc, win_state_t, win_new, ov, expand)


def _attn_sample_b_kernel(pt_ref, *refs, pps):
    pages = refs[:pps]
    (q_ref, kvn_ref, bias_ref, ocmp_ref, owin_ref, gates_ref, o_ref, s_ref, vt_ref) = refs[pps:]
    t = q_ref.shape[1]
    page = pages[0].shape[3]
    n = pps * page
    nr = N_Q_HEADS
    n_steps = s_ref.shape[0]
    step = pl.program_id(1)
    lane = _iota((t, LANE), 1)
    qbd = jnp.concatenate(_q_rows(q_ref[0], lane), axis=0)
    kt = jnp.concatenate([pages[k][0, 0, 0:KV_W, :].astype(BF16) for k in range(pps)], axis=1)
    s_ref[step] = _dot(qbd, kt)
    for k in range(pps):
        vt_ref[step, :, k * page:(k + 1) * page] = pages[k][0, 0, KV_W:2 * KV_W, :].astype(BF16)

    @pl.when(step == n_steps - 1)
    def _():
        biased = lambda s_flat, bias: (s_flat.reshape(N_KV_HEADS, Q_PER_KV, t, s_flat.shape[-1])
                                       + bias[:, None]).reshape(nr, t, s_flat.shape[-1])
        parts = [biased(s_ref[j], bias_ref[0, :, :, j * n:(j + 1) * n]) for j in range(n_steps)]
        kvn = _pad_rows(kvn_ref[0], NEW_PAD).astype(BF16)
        causal = _iota((t, NEW_PAD), 1) <= _iota((t, 1), 0)
        bias_new = jnp.where(causal[None], bias_ref[0, :, :, n_steps * n:n_steps * n + NEW_PAD], NEG)
        s_new = biased(_dot_nt(qbd, kvn[:, 0:KV_W]), bias_new)
        m = jnp.max(s_new, axis=-1, keepdims=True)
        for p in parts:
            m = jnp.maximum(m, jnp.max(p, axis=-1, keepdims=True))
        e = jnp.exp2(s_new - m)
        l = jnp.sum(e, axis=-1, keepdims=True)
        acc = _dot(e.reshape(nr * t, NEW_PAD).astype(BF16), kvn[:, KV_W:2 * KV_W])
        for j, p in enumerate(parts):
            e = jnp.exp2(p - m)
            l = l + jnp.sum(e, axis=-1, keepdims=True)
            acc = acc + _dot_nt(e.reshape(nr * t, n).astype(BF16), vt_ref[j])
        o_slc = acc.reshape(nr, t, LANE) / l
        gates = gates_ref[0]
        vals = [_gate_combine(gates, h, ocmp_ref[0, h * t:(h + 1) * t], o_slc[h], owin_ref[0, h * t:(h + 1) * t])
                for h in range(nr)]
        o_ref[0] = _assemble_heads(vals, lane)


def _attn_sample_b(cache_t, layer, page_table, q, kv_new, bias, o_cmp, o_win, gates, pps):
    bx, n_pages = page_table.shape
    page = cache_t.shape[3]
    t = q.shape[1]
    rows = N_Q_HEADS * t
    n = pps * page
    page_specs = [pl.BlockSpec((1, 1, 2 * KV_W, page),
                               lambda b, s, pt, k=k: (layer, pt[b, s * pps + k], 1, 0))
                  for k in range(pps)]
    return pl.pallas_call(
        functools.partial(_attn_sample_b_kernel, pps=pps),
        grid_spec=pltpu.PrefetchScalarGridSpec(
            num_scalar_prefetch=1,
            grid=(bx, n_pages // pps),
            in_specs=page_specs + [
                pl.BlockSpec((1, t, ATTN_W), lambda b, s, pt: (b, 0, 0)),
                pl.BlockSpec((1, t, 2 * KV_W), lambda b, s, pt: (b, 0, 1)),
                pl.BlockSpec((1,) + bias.shape[1:], lambda b, s, pt: (b, 0, 0, 0)),
                pl.BlockSpec((1, rows, LANE), lambda b, s, pt: (b, 0, 0)),
                pl.BlockSpec((1, rows, LANE), lambda b, s, pt: (b, 0, 0)),
                pl.BlockSpec((1, t, GATE_PAD), lambda b, s, pt: (b, 0, 0))],
            out_specs=pl.BlockSpec((1, t, ATTN_W), lambda b, s, pt: (b, 0, 0)),
            scratch_shapes=[pltpu.VMEM((n_pages // pps, rows, n), F32),
                            pltpu.VMEM((n_pages // pps, KV_W, n), BF16)]),
        out_shape=jax.ShapeDtypeStruct((bx, t, ATTN_W), F32),
        compiler_params=_params("parallel", "arbitrary"),
        name="attn_sample_b",
    )(page_table, *([cache_t] * pps), q, kv_new, bias, o_cmp, o_win, gates)


def _mix_out_kernel(x_ref, mod_ref, oattn_ref, bg_ref, u_ref, uh_ref, p_ref, ph_ref, cw_ref, cb_ref,
                    pw_ref, ps_ref, wo_ref, norm_ref, x1_ref, h2_ref, *, pos0, zero_first_halo):
    bb, tt, d = x_ref.shape
    rows = bb * tt
    i = pl.program_id(1)
    uh = uh_ref[...]
    ph = ph_ref[...]
    if zero_first_halo:
        keep = jnp.where(i > 0, 1.0, 0.0)
        uh = uh * keep
        ph = ph * keep
    ucat = jnp.concatenate([uh, u_ref[...]], axis=1)
    conv = cb_ref[...]
    for j in range(CONV_K):
        off = CONV_HALO - CONV_BUF + j
        conv = conv + ucat[:, off:off + tt] * cw_ref[j:j + 1]
    y_conv = bg_ref[...] * conv
    pcat = jnp.concatenate([ph, p_ref[...]], axis=1)
    a2 = pcat[:, 1:] + pcat[:, :-1]
    a4 = a2[:, 2:] + a2[:, :-2]
    a8 = a4[:, 4:] + a4[:, :-4]
    a16 = a8[:, 8:] + a8[:, :-8]
    sums = (a2[:, POOL_HALO - 1:POOL_HALO - 1 + tt], a4[:, POOL_HALO - 3:POOL_HALO - 3 + tt],
            a8[:, POOL_HALO - 7:POOL_HALO - 7 + tt], a16[:, POOL_HALO - 15:POOL_HALO - 15 + tt])
    pos = pos0 + i * tt + _iota((1, tt, 1), 1)
    grp = _iota((1, 1, POOL_W), 2) // (POOL_W // len(POOL_WINDOWS))
    mean = jnp.zeros((bb, tt, POOL_W), F32)
    for gi, w in enumerate(POOL_WINDOWS):
        cnt = jnp.minimum(w, pos + 1).astype(F32)
        mean = jnp.where(grp == gi, sums[gi] / cnt, mean)
    dlt = (mean - p_ref[...]).reshape(rows, POOL_W).astype(BF16)
    y_pool = _dot(dlt, pw_ref[...]) * ps_ref[...]
    mix = _dot(oattn_ref[...].reshape(rows, ATTN_W).astype(BF16), wo_ref[0, 0:ATTN_W])
    mix = mix + _dot(y_conv.reshape(rows, CONV_W).astype(BF16), wo_ref[0, ATTN_W:ATTN_W + CONV_W])
    mix = mix + _dot(y_pool.astype(BF16), wo_ref[0, ATTN_W + CONV_W:ATTN_W + CONV_W + POOL_W])
    gate1 = mod_ref[:, :, 2 * d:3 * d]
    x1 = x_ref[...] + gate1 * mix.reshape(bb, tt, d)
    x1_ref[...] = x1
    y = x1 * lax.rsqrt(jnp.mean(x1 * x1, axis=-1, keepdims=True) + EPS) * norm_ref[...]
    h2_ref[...] = (y * (1.0 + mod_ref[:, :, 4 * d:5 * d]) + mod_ref[:, :, 3 * d:4 * d]).astype(h2_ref.dtype)


def _mix_out(x, mod, oattn, bg, u, u_halo, pool, p_halo, conv_w, conv_b, pool_wbd, pool_scale, w_out, layer, norm2,
             bb, tt, pos0, halo_from_self):
    bx, t, d = x.shape
    wo_spec = pl.BlockSpec((1,) + w_out.shape[1:], lambda b, i: (layer, 0, 0), pipeline_mode=pl.Buffered(1))
    blk = lambda w: pl.BlockSpec((bb, tt, w), lambda b, i: (b, i, 0))
    full = lambda a: pl.BlockSpec(a.shape, lambda b, i: (0,) * a.ndim)
    if halo_from_self:
        uh_spec = pl.BlockSpec((bb, CONV_HALO, CONV_W),
                               lambda b, i: (b, jnp.maximum(i * (tt // CONV_HALO) - 1, 0), 0))
        ph_spec = pl.BlockSpec((bb, POOL_HALO, POOL_W),
                               lambda b, i: (b, jnp.maximum(i * (tt // POOL_HALO) - 1, 0), 0))
    else:
        uh_spec = pl.BlockSpec((bb, CONV_HALO, CONV_W), lambda b, i: (b, 0, 0))
        ph_spec = pl.BlockSpec((bb, POOL_HALO, POOL_W), lambda b, i: (b, 0, 0))
    return pl.pallas_call(
        functools.partial(_mix_out_kernel, pos0=pos0, zero_first_halo=halo_from_self),
        grid=(bx // bb, t // tt),
        in_specs=[blk(d), pl.BlockSpec((bb, 1, mod.shape[-1]), lambda b, i: (b, 0, 0)),
                  blk(ATTN_W), blk(CONV_W), blk(CONV_W), uh_spec, blk(POOL_W), ph_spec,
                  full(conv_w), full(conv_b), full(pool_wbd), full(pool_scale), wo_spec, full(norm2)],
        out_specs=[blk(d), blk(d)],
        out_shape=[jax.ShapeDtypeStruct((bx, t, d), F32), jax.ShapeDtypeStruct((bx, t, d), oattn.dtype)],
        compiler_params=_params("parallel", "parallel"),
        name="mix_out",
    )(x, mod, oattn, bg, u, u_halo, pool, p_halo, conv_w, conv_b, pool_wbd, pool_scale, w_out, norm2)


def _ffn_kernel(x1_ref, h2_ref, mod_ref, wa_ref, wb_ref, wd_ref, o_ref):
    bb, tt, d = x1_ref.shape
    h = h2_ref[...].reshape(bb * tt, d).astype(BF16)
    a = _dot(h, wa_ref[0])
    b = _dot(h, wb_ref[0])
    y = _dot((_silu(a) * b).astype(BF16), wd_ref[0])
    o_ref[...] = x1_ref[...] + mod_ref[:, :, 5 * d:6 * d] * y.reshape(bb, tt, d)


def _ffn(x1, h2, mod, w_up, w_down, layer, bb, tt):
    bx, t, d = x1.shape
    hid = w_down.shape[1]
    blk = pl.BlockSpec((bb, tt, d), lambda b, i: (b, i, 0))
    resident = lambda shape, imap: pl.BlockSpec(shape, imap, pipeline_mode=pl.Buffered(1))
    return pl.pallas_call(
        _ffn_kernel,
        grid=(bx // bb, t // tt),
        in_specs=[blk, blk, pl.BlockSpec((bb, 1, mod.shape[-1]), lambda b, i: (b, 0, 0)),
                  resident((1, d, hid), lambda b, i: (layer, 0, 0)),
                  resident((1, d, hid), lambda b, i: (layer, 0, 1)),
                  resident((1, hid, d), lambda b, i: (layer, 0, 0))],
        out_specs=blk,
        out_shape=jax.ShapeDtypeStruct((bx, t, d), F32),
        compiler_params=_params("parallel", "parallel"),
        name="ffn",
    )(x1, h2, mod, w_up, w_up, w_down)


def _overlap_matrix(n_cmp, n_sel, rows, lanes):
    m = np.zeros((rows, lanes), np.float32)
    i = np.arange(n_cmp)
    for part in range(CMP_BLOCK // CMP_STRIDE):
        j = np.minimum((i + part) * CMP_STRIDE // SLC_BLOCK, n_sel - 1)
        np.add.at(m, (i, j), 1.0)
    return m


def _expand_matrix(n_blocks, n_keys):
    return (np.arange(n_blocks)[:, None] == np.arange(n_keys)[None, :] // SLC_BLOCK).astype(np.float32)


def _chunk_order_perm(page):
    pos = np.arange(page)
    m = np.zeros((page, page), np.float32)
    m[(pos % CMP_STRIDE) * (page // CMP_STRIDE) + pos // CMP_STRIDE, pos] = 1.0
    return jnp.asarray(np.stack([m, m.T]), BF16)


def _seg_matrix(n):
    idx = np.arange(n) // HEAD_DIM
    return jnp.asarray((idx[:, None] == idx[None, :]).astype(np.float32) / HEAD_DIM, BF16)


def _prep_weights(w_in, w_out, q_norm, k_norm, cmp_pe, cmp_w1, cmp_w2, pool_w, w_up, w_down):
    depth, d, _ = w_in.shape
    g0 = ATTN_W + 6 * KV_W
    w_in_r = jnp.concatenate([w_in[:, :, :g0], w_in[:, :, g0 + N_Q_HEADS * N_BRANCH:],
                              w_in[:, :, g0:g0 + N_Q_HEADS * N_BRANCH],
                              jnp.zeros((depth, d, GATE_PAD - N_Q_HEADS * N_BRANCH), w_in.dtype)], axis=-1)
    eye_g = jnp.eye(N_KV_HEADS, dtype=BF16)
    w1r = cmp_w1.astype(BF16).reshape(depth, 2, 2, CMP_STRIDE, HEAD_DIM, HEAD_DIM)
    w1t = jnp.transpose(w1r, (0, 1, 3, 4, 2, 5))
    wr = w1t[:, :, :, None, :, :, None, :] * eye_g[None, None, None, :, None, None, :, None]
    wr = wr.reshape(depth, 2, CMP_STRIDE * KV_W, 2 * KV_W)
    w2bd = (cmp_w2.astype(BF16)[:, :, None, :, None, :]
            * eye_g[None, None, :, None, :, None]).reshape(depth, 2, KV_W, KV_W)
    pe = cmp_pe.reshape(depth, 2, 2, CMP_STRIDE, 1, HEAD_DIM)
    pe = jnp.broadcast_to(pe, (depth, 2, 2, CMP_STRIDE, N_KV_HEADS, HEAD_DIM))
    pe = pe.reshape(depth, 2, 2, CMP_STRIDE * KV_W)
    pe_aug = jnp.concatenate([pe, jnp.zeros((depth, 2, 6, CMP_STRIDE * KV_W), F32)], axis=2)
    eye_p = jnp.eye(len(POOL_WINDOWS), dtype=F32)
    pool_bd = (pool_w[:, :, :, None, :] * eye_p[None, :, None, :, None]).reshape(depth, POOL_W, POOL_W)
    return dict(
        w_in=w_in_r.astype(BF16), w_out=w_out.astype(BF16), w_up=w_up.astype(BF16), w_down=w_down.astype(BF16),
        wr=wr.astype(BF16), pe_aug=pe_aug, w2bd=w2bd.astype(BF16), pool_bd=pool_bd.astype(BF16),
        q_gain=jnp.tile(q_norm, (1, N_Q_HEADS))[:, None, :],
        k_gain=jnp.tile(k_norm, (1, 1, N_KV_HEADS)),
    )


def _feature_major_view(a):
    lead = a.shape[:-2]
    f, p = a.shape[-2:]
    a = a.reshape(lead + (f // KV_W, N_KV_HEADS, HEAD_DIM, p))
    n = len(lead)
    return jnp.transpose(a, tuple(range(n)) + (n + 3, n, n + 1, n + 2))


def _layer_prompt(x, mod, l, W, P, C):
    bx, t, d = x.shape
    bb, tt = 1, ROW_TILE
    kg = P["k_gain"][l]
    q, bg, u, pool, gates, kv_t, win_t, raw, ksa, vst, kw, vwt = _proj_in(
        x, mod, W["norm_mix"][l][None, None], P["w_in"], l, P["q_gain"][l], kg[1:3],
        C["seg_q"], C["seg_k"], bb, tt, True)
    kc, vct = _cmp_prompt(raw, P["wr"][l], P["pe_aug"][l], P["w2bd"][l], C["seg_k"], kg[0:1])
    oattn = _attn_prompt(q, kc, vct, ksa, vst, kw, vwt, gates, C["ovt_prompt"])
    x1, h2 = _mix_out(x, mod, oattn, bg, u, u, pool, pool, W["conv_w"][l], W["conv_bias"][l][None],
                      P["pool_bd"][l], W["pool_scale"][l][None], P["w_out"], l, W["norm_ffn"][l][None, None],
                      bb, tt, 0, True)
    x2 = _ffn(x1, h2, mod, P["w_up"], P["w_down"], l, bb, tt)
    states = (kv_t, win_t[:, :, t - min(WINDOW, t):], u[:, t - CONV_BUF:], pool[:, t - POOL_BUF:])
    return x2, states


def _layer_sample(x, mod, l, W, P, C, cache_t, page_table, win_state_t, conv_state, pool_state):
    bx, t, d = x.shape
    n_pages = page_table.shape[1]
    pos0 = n_pages * cache_t.shape[3]
    pps = C["pps"]
    kg = P["k_gain"][l]
    q, bg, u, pool, gates, kv, win = _proj_in(
        x, mod, W["norm_mix"][l][None, None], P["w_in"], l, P["q_gain"][l], kg[1:3],
        C["seg_q"], C["seg_k"], bx, t, False)
    kc, vc = _cmp_sample(cache_t, l, page_table, C["perm"], P["wr"][l], P["pe_aug"][l], P["w2bd"][l],
                         C["seg_k"], kg[0:1], pps)
    o_cmp, o_win, bias = _attn_sample_a(q, kc, vc, win_state_t, l, win, C["ov_sample"], C["expand_sample"],
                                        pos0, C["n_cmp_s"], C["n_sel_s"])
    oattn = _attn_sample_b(cache_t, l, page_table, q, kv, bias, o_cmp, o_win, gates, pps)
    u_halo = jnp.concatenate([jnp.zeros((bx, CONV_HALO - CONV_BUF, CONV_W), F32), conv_state], axis=1)
    p_halo = jnp.concatenate([jnp.zeros((bx, POOL_HALO - POOL_BUF, POOL_W), F32), pool_state], axis=1)
    x1, h2 = _mix_out(x, mod, oattn, bg, u, u_halo, pool, p_halo, W["conv_w"][l], W["conv_bias"][l][None],
                      P["pool_bd"][l], W["pool_scale"][l][None], P["w_out"], l, W["norm_ffn"][l][None, None],
                      bx, t, pos0, False)
    x2 = _ffn(x1, h2, mod, P["w_up"], P["w_down"], l, bx, t)
    win_t = jnp.concatenate([win_state_t[l][:, :, t:], jnp.swapaxes(win, 1, 2)], axis=2)
    states = (kv.reshape(bx, t, 4, N_KV_HEADS, HEAD_DIM), win_t,
              jnp.concatenate([conv_state, u], axis=1)[:, t:],
              jnp.concatenate([pool_state, pool], axis=1)[:, t:])
    return x2, states


def kernel(x_prompt, x_sample, cache_nsa_kv, state_win_kv, state_conv, state_pool, page_table, c_prompt, c_sample, norm_mix, norm_ffn, w_ada, b_ada, w_in, w_out, q_norm, k_norm, cmp_pe, cmp_w1, cmp_w2, conv_w, conv_bias, pool_w, pool_scale, w_up, w_down):
    depth = w_in.shape[0]
    bp, tp, d = x_prompt.shape
    bs, ts, _ = x_sample.shape
    n_pages = page_table.shape[1]
    n_pool, page = cache_nsa_kv.shape[1:3]
    past = n_pages * page
    n_state = state_win_kv.shape[2]
    hid = w_down.shape[1]
    assert tp % ROW_TILE == 0 and tp >= WINDOW + TQ and past % SLC_BLOCK == 0 and ts <= CONV_HALO

    W = dict(norm_mix=norm_mix, norm_ffn=norm_ffn, conv_w=conv_w, conv_bias=conv_bias, pool_scale=pool_scale)
    P = _prep_weights(w_in, w_out, q_norm, k_norm, cmp_pe, cmp_w1, cmp_w2, pool_w, w_up, w_down)
    n_cmp_p = (tp - CMP_BLOCK) // CMP_STRIDE + 1
    n_sel_p = tp // SLC_BLOCK
    n_cmp_s = (past + ts - CMP_BLOCK) // CMP_STRIDE + 1
    n_sel_s = -(-(past + ts) // SLC_BLOCK)
    sel_lanes = -(-n_sel_s // LANE) * LANE
    pps = min(PAGES_PER_STEP, n_pages)
    C = dict(
        seg_q=_seg_matrix(ATTN_W), seg_k=_seg_matrix(KV_W), perm=_chunk_order_perm(page),
        ovt_prompt=jnp.asarray(_overlap_matrix(n_cmp_p, n_sel_p, tp // CMP_STRIDE, n_sel_p).T, BF16),
        ov_sample=jnp.asarray(_overlap_matrix(n_cmp_s, n_sel_s, past // CMP_STRIDE, sel_lanes), BF16),
        expand_sample=jnp.asarray(_expand_matrix(sel_lanes, past + NEW_PAD), BF16),
        n_cmp_s=n_cmp_s, n_sel_s=n_sel_s, pps=pps,
    )
    cache_t = jnp.transpose(cache_nsa_kv, (0, 1, 3, 4, 5, 2)).reshape(depth, n_pool, 4 * KV_W, page)
    win_state_t = jnp.transpose(state_win_kv, (0, 1, 3, 4, 5, 2)).reshape(depth, bs, 2 * KV_W, n_state)
    mod = _ada(jnp.concatenate([c_prompt, c_sample], axis=0), w_ada, b_ada)
    mod_p = mod[:, :bp, None, :]
    mod_s = mod[:, bp:, None, :]

    yp, ys = x_prompt, x_sample
    outs_p, outs_s = [], []
    for l in range(depth):
        yp, st = _layer_prompt(yp, mod_p[l], l, W, P, C)
        outs_p.append(st)
        ys, st = _layer_sample(ys, mod_s[l], l, W, P, C, cache_t, page_table, win_state_t,
                               state_conv[l], state_pool[l])
        outs_s.append(st)
    stack = lambda outs, k: jnp.stack([o[k] for o in outs])
    return (yp, ys, _feature_major_view(stack(outs_p, 0)), stack(outs_s, 0),
            _feature_major_view(stack(outs_p, 1)), _feature_major_view(stack(outs_s, 1)),
            stack(outs_p, 2), stack(outs_s, 2), stack(outs_p, 3), stack(outs_s, 3))
```

```python
import functools

import numpy as np
import jax
import jax.numpy as jnp
from jax import lax
from jax.experimental import pallas as pl
from jax.experimental.pallas import tpu as pltpu

F32 = jnp.float32
BF16 = jnp.bfloat16

HEAD_DIM = 64
N_Q_HEADS = 8
N_KV_HEADS = 2
Q_PER_KV = N_Q_HEADS // N_KV_HEADS
N_BRANCH = 3
ATTN_W = N_Q_HEADS * HEAD_DIM
KV_W = N_KV_HEADS * HEAD_DIM
CONV_W = 256
POOL_W = 256
CONV_K = 3
CONV_BUF = CONV_K - 1
POOL_WINDOWS = (2, 4, 8, 16)
POOL_BUF = max(POOL_WINDOWS) - 1
CMP_BLOCK = 32
CMP_STRIDE = 16
SLC_BLOCK = 64
N_SLC = 16
WINDOW = 512
EPS = 1e-6
NEG = -1e30
FORCE = 1e4
ATTN_SCALE = HEAD_DIM ** -0.5
LOG2E = 1.4426950408889634
LANE = 128
GATE_PAD = LANE
IN_W_PAD = ATTN_W + 4 * KV_W + 2 * KV_W + 3 * CONV_W + POOL_W + GATE_PAD
CONV_HALO = 8
POOL_HALO = 16
NEW_PAD = LANE
VMEM_LIMIT = 56 * 1024 * 1024
ROW_TILE = 512
TQ = 512
SLC_CHUNK = 512
PAGES_PER_STEP = 64
ADA_TN = 1536

_NT = (((1,), (1,)), ((), ()))


def _params(*sem):
    return pltpu.CompilerParams(dimension_semantics=sem, vmem_limit_bytes=VMEM_LIMIT)


def _dot(a, b):
    return jnp.dot(a, b, preferred_element_type=F32)


def _dot_nt(a, b):
    return lax.dot_general(a, b, _NT, preferred_element_type=F32)


def _split_hilo(x):
    hi = x.astype(BF16)
    return hi, (x - hi.astype(F32)).astype(BF16)


def _dot_hilo(x, w_bf):
    hi, lo = _split_hilo(x)
    return _dot(hi, w_bf) + _dot(lo, w_bf)


def _seg_rms(x, seg_bf, gain):
    ms = _dot_hilo(x * x, seg_bf)
    return x * lax.rsqrt(ms + EPS) * gain


def _silu(x):
    return x * jax.nn.sigmoid(x)


def _pad_rows(x, n):
    return jnp.concatenate([x, jnp.zeros((n - x.shape[0], x.shape[1]), x.dtype)], axis=0)


def _iota(shape, axis):
    return lax.broadcasted_iota(jnp.int32, shape, axis)


def _ada_kernel(c_ref, w_ref, b_ref, o_ref):
    c = c_ref[...]
    o_ref[0] = _dot(_silu(c).astype(BF16), w_ref[0].astype(BF16)) + b_ref[0]


def _ada(c_all, w_ada, b_ada):
    depth, d, n = w_ada.shape
    r = c_all.shape[0]
    return pl.pallas_call(
        _ada_kernel,
        grid=(depth, n // ADA_TN),
        in_specs=[pl.BlockSpec((r, d), lambda l, j: (0, 0)),
                  pl.BlockSpec((1, d, ADA_TN), lambda l, j: (l, 0, j)),
                  pl.BlockSpec((1, 1, ADA_TN), lambda l, j: (l, 0, j))],
        out_specs=pl.BlockSpec((1, r, ADA_TN), lambda l, j: (l, 0, j)),
        out_shape=jax.ShapeDtypeStruct((depth, r, n), F32),
        compiler_params=_params("parallel", "parallel"),
        name="ada",
    )(c_all, w_ada, b_ada.reshape(depth, 1, n))


def _proj_in_kernel(x_ref, mod_ref, norm_ref, w_ref, qg_ref, kg_ref, seg_q_ref, seg_k_ref,
                    q_ref, bg_ref, u_ref, pool_ref, gates_ref, *kv_refs, feature_major):
    bb, tt, d = x_ref.shape
    rows = bb * tt
    x = x_ref[...]
    y = x * lax.rsqrt(jnp.mean(x * x, axis=-1, keepdims=True) + EPS) * norm_ref[...]
    shift1 = mod_ref[:, :, 0:d]
    scale1 = mod_ref[:, :, d:2 * d]
    h = (y * (1.0 + scale1) + shift1).reshape(rows, d).astype(BF16)
    z = _dot(h, w_ref[0])

    qn = _seg_rms(z[:, 0:ATTN_W], seg_q_ref[...], qg_ref[...]) * (ATTN_SCALE * LOG2E)
    q_ref[...] = qn.reshape(bb, tt, ATTN_W)
    o = ATTN_W
    raw_cmp = z[:, o:o + 2 * KV_W]
    ks = _seg_rms(z[:, o + 2 * KV_W:o + 3 * KV_W], seg_k_ref[...], kg_ref[0:1])
    vs = z[:, o + 3 * KV_W:o + 4 * KV_W]
    kv = jnp.concatenate([raw_cmp, ks, vs], axis=-1)
    o += 4 * KV_W
    kw = _seg_rms(z[:, o:o + KV_W], seg_k_ref[...], kg_ref[1:2])
    vw = z[:, o + KV_W:o + 2 * KV_W]
    win = jnp.concatenate([kw, vw], axis=-1)
    o += 2 * KV_W
    if feature_major:
        kvt_ref, wint_ref, raw_ref, ksa_ref, vst_ref, kw_ref, vwt_ref = kv_refs
        kvt = kv.T
        wint = win.T
        kvt_ref[0] = kvt
        wint_ref[0] = wint
        raw_ref[0] = raw_cmp
        key_block = (pl.program_id(1) * rows + _iota((rows, LANE), 0)) // SLC_BLOCK
        onehot = jnp.where(key_block == _iota((rows, LANE), 1), 1.0, 0.0)
        ksa_ref[0] = jnp.concatenate([ks, onehot], axis=-1).astype(BF16)
        vst = kvt[3 * KV_W:4 * KV_W].astype(BF16)
        for j in range(rows // SLC_CHUNK):
            vst_ref[0, j] = vst[:, j * SLC_CHUNK:(j + 1) * SLC_CHUNK]
        kw_ref[0] = kw.astype(BF16)
        vwt = wint[KV_W:2 * KV_W].astype(BF16)
        for j in range(rows // LANE):
            vwt_ref[0, j] = vwt[:, j * LANE:(j + 1) * LANE]
    else:
        kv_ref, win_ref = kv_refs
        kv_ref[...] = kv.reshape(bb, tt, 4 * KV_W)
        win_ref[...] = win.reshape(bb, tt, 2 * KV_W)
    bg_ref[...] = z[:, o:o + CONV_W].reshape(bb, tt, CONV_W)
    u_ref[...] = (z[:, o + CONV_W:o + 2 * CONV_W] * z[:, o + 2 * CONV_W:o + 3 * CONV_W]).reshape(bb, tt, CONV_W)
    o += 3 * CONV_W
    pool_ref[...] = z[:, o:o + POOL_W].reshape(bb, tt, POOL_W)
    o += POOL_W
    gates_ref[...] = jax.nn.sigmoid(z[:, o:o + GATE_PAD]).reshape(bb, tt, GATE_PAD)


def _proj_in(x, mod, norm1, w_in, layer, q_gain, k_gain2, seg_q, seg_k, bb, tt, feature_major):
    bx, t, d = x.shape
    w_spec = pl.BlockSpec((1,) + w_in.shape[1:], lambda b, i: (layer, 0, 0), pipeline_mode=pl.Buffered(1))
    grid = (bx // bb, t // tt)
    blk = lambda w: pl.BlockSpec((bb, tt, w), lambda b, i: (b, i, 0))
    full = lambda a: pl.BlockSpec(a.shape, lambda b, i: (0,) * a.ndim)
    outs = [((bx, t, ATTN_W), F32), ((bx, t, CONV_W), F32), ((bx, t, CONV_W), F32), ((bx, t, POOL_W), F32),
            ((bx, t, GATE_PAD), F32)]
    out_specs = [blk(s[-1]) for s, _ in outs]
    if feature_major:
        assert bb == 1 and tt % SLC_CHUNK == 0 and SLC_CHUNK % LANE == 0
        outs += [((bx, 4 * KV_W, t), F32), ((bx, 2 * KV_W, t), F32), ((bx, t, 2 * KV_W), F32),
                 ((bx, t, 2 * KV_W), BF16), ((bx, t // SLC_CHUNK, KV_W, SLC_CHUNK), BF16),
                 ((bx, t, KV_W), BF16), ((bx, t // LANE, KV_W, LANE), BF16)]
        out_specs += [pl.BlockSpec((1, 4 * KV_W, tt), lambda b, i: (b, 0, i)),
                      pl.BlockSpec((1, 2 * KV_W, tt), lambda b, i: (b, 0, i)),
                      blk(2 * KV_W),
                      blk(2 * KV_W),
                      pl.BlockSpec((1, tt // SLC_CHUNK, KV_W, SLC_CHUNK), lambda b, i: (b, i, 0, 0)),
                      blk(KV_W),
                      pl.BlockSpec((1, tt // LANE, KV_W, LANE), lambda b, i: (b, i, 0, 0))]
    else:
        outs += [((bx, t, 4 * KV_W), F32), ((bx, t, 2 * KV_W), F32)]
        out_specs += [blk(4 * KV_W), blk(2 * KV_W)]
    return pl.pallas_call(
        functools.partial(_proj_in_kernel, feature_major=feature_major),
        grid=grid,
        in_specs=[blk(d), pl.BlockSpec((bb, 1, mod.shape[-1]), lambda b, i: (b, 0, 0)),
                  full(norm1), w_spec, full(q_gain), full(k_gain2), full(seg_q), full(seg_k)],
        out_specs=out_specs,
        out_shape=[jax.ShapeDtypeStruct(s, dt) for s, dt in outs],
        compiler_params=_params("parallel", "parallel"),
        name="proj_in",
    )(x, mod, norm1, w_in, q_gain, k_gain2, seg_q, seg_k)


def _compress_core(chunks, nc, wr_ref, pe_ref, w2_ref, seg_k_ref, kg0_ref):
    accs = [_dot(chunks(s).astype(BF16), wr_ref[s]) for s in range(2)]
    outs = []
    for s in range(2):
        acc = accs[s]
        acc_pe = _dot(pe_ref[s].astype(BF16), wr_ref[s])
        bias = acc_pe[0:1, 0:KV_W] + acc_pe[1:2, KV_W:2 * KV_W]
        nxt = pltpu.roll(acc[:, KV_W:2 * KV_W], nc - 1, 0)
        pre = acc[:, 0:KV_W] + nxt + bias
        outs.append(_dot(_silu(pre).astype(BF16), w2_ref[s]))
    return _seg_rms(outs[0], seg_k_ref[...], kg0_ref[...]), outs[1]


def _cmp_prompt_kernel(k_ref, v_ref, wr_ref, pe_ref, w2_ref, seg_k_ref, kg0_ref, kc_ref, vc_ref):
    nc = kc_ref.shape[1]
    src = (k_ref, v_ref)
    read = lambda s: jnp.concatenate(
        [src[s][0, pl.ds(r, nc, stride=CMP_STRIDE), :] for r in range(CMP_STRIDE)], axis=1)
    kc, vc = _compress_core(read, nc, wr_ref, pe_ref, w2_ref, seg_k_ref, kg0_ref)
    kc_ref[0] = kc.astype(BF16)
    vc_ref[0] = vc.T.astype(BF16)


def _cmp_prompt(raw, wr, pe_aug, w2bd, seg_k, kg0):
    bx, t, _ = raw.shape
    nc = t // CMP_STRIDE
    full = lambda a: pl.BlockSpec(a.shape, lambda b: (0,) * a.ndim)
    return pl.pallas_call(
        _cmp_prompt_kernel,
        grid=(bx,),
        in_specs=[pl.BlockSpec((1, t, KV_W), lambda b: (b, 0, 0)),
                  pl.BlockSpec((1, t, KV_W), lambda b: (b, 0, 1)),
                  full(wr), full(pe_aug), full(w2bd), full(seg_k), full(kg0)],
        out_specs=[pl.BlockSpec((1, nc, KV_W), lambda b: (b, 0, 0)), pl.BlockSpec((1, KV_W, nc), lambda b: (b, 0, 0))],
        out_shape=[jax.ShapeDtypeStruct((bx, nc, KV_W), BF16), jax.ShapeDtypeStruct((bx, KV_W, nc), BF16)],
        compiler_params=_params("parallel"),
        name="cmp_prompt",
    )(raw, raw, wr, pe_aug, w2bd, seg_k, kg0)


def _cmp_sample_kernel(pt_ref, *refs, pps):
    pages = refs[:pps]
    perm_ref, wr_ref, pe_ref, w2_ref, seg_k_ref, kg0_ref, kc_ref, vc_ref, xs_ref = refs[pps:]
    step = pl.program_id(1)
    cpp = pages[0].shape[3] // CMP_STRIDE
    reorder = (lambda tile: _dot_nt(perm_ref[0], tile), lambda tile: _dot(tile, perm_ref[1]).T)
    for k in range(0, pps, 2):
        row0 = pl.multiple_of((step * pps + k) * cpp, 2 * cpp)
        for s in range(2):
            zt = [reorder[s](pages[k + j][0, 0, s * KV_W:(s + 1) * KV_W, :].astype(BF16))
                  for j in range(2)]
            for r in range(CMP_STRIDE):
                pair = jnp.concatenate([zt[0][r * cpp:(r + 1) * cpp], zt[1][r * cpp:(r + 1) * cpp]], axis=0)
                xs_ref[s, pl.ds(row0, 2 * cpp), r * KV_W:(r + 1) * KV_W] = pair.astype(BF16)

    @pl.when(step == pl.num_programs(1) - 1)
    def _():
        nc = kc_ref.shape[1]
        read = lambda s: xs_ref[s]
        kc, vc = _compress_core(read, nc, wr_ref, pe_ref, w2_ref, seg_k_ref, kg0_ref)
        kc_ref[0] = kc.astype(BF16)
        vc_ref[0] = vc.astype(BF16)


def _cmp_sample(cache_t, layer, page_table, perm, wr, pe_aug, w2bd, seg_k, kg0, pps):
    bx, n_pages = page_table.shape
    page = cache_t.shape[3]
    past = n_pages * page
    nc = past // CMP_STRIDE
    full = lambda a: pl.BlockSpec(a.shape, lambda b, s, pt: (0,) * a.ndim)
    page_specs = [pl.BlockSpec((1, 1, 2 * KV_W, page),
                               lambda b, s, pt, k=k: (layer, pt[b, s * pps + k], 0, 0))
                  for k in range(pps)]
    return pl.pallas_call(
        functools.partial(_cmp_sample_kernel, pps=pps),
        grid_spec=pltpu.PrefetchScalarGridSpec(
            num_scalar_prefetch=1,
            grid=(bx, n_pages // pps),
            in_specs=page_specs + [full(perm), full(wr), full(pe_aug), full(w2bd), full(seg_k), full(kg0)],
            out_specs=[pl.BlockSpec((1, nc, KV_W), lambda b, s, pt: (b, 0, 0))] * 2,
            scratch_shapes=[pltpu.VMEM((2, nc, CMP_STRIDE * KV_W), BF16)]),
        out_shape=[jax.ShapeDtypeStruct((bx, nc, KV_W), BF16)] * 2,
        compiler_params=_params("parallel", "arbitrary"),
        name="cmp_sample",
    )(page_table, *([cache_t] * pps), perm, wr, pe_aug, w2bd, seg_k, kg0)


def _q_rows(q, lane):
    out = []
    for g in range(N_KV_HEADS):
        keep = (lane // HEAD_DIM) == g
        rows = []
        for r in range(Q_PER_KV):
            h = g * Q_PER_KV + r
            blk = q[:, (h // 2) * LANE:(h // 2 + 1) * LANE]
            if h % 2 != g:
                blk = pltpu.roll(blk, HEAD_DIM, 1)
            rows.append(jnp.where(keep, blk, 0.0))
        out.append(jnp.concatenate(rows, axis=0).astype(BF16))
    return out


def _assemble_heads(vals, lane):
    cols = []
    for k in range(N_Q_HEADS // 2):
        a, b = vals[2 * k], vals[2 * k + 1]
        if (2 * k) // Q_PER_KV == 1:
            a = pltpu.roll(a, HEAD_DIM, 1)
        else:
            b = pltpu.roll(b, HEAD_DIM, 1)
        cols.append(jnp.where(lane < HEAD_DIM, a, b))
    return jnp.concatenate(cols, axis=-1)


def _gate_combine(gates, h, o_cmp, o_slc, o_win):
    c = h * N_BRANCH
    return gates[:, c:c + 1] * o_cmp + gates[:, c + 1:c + 2] * o_slc + gates[:, c + 2:c + 3] * o_win


def _softmax_parts(s3):
    m = jnp.max(s3, axis=-1, keepdims=True)
    e = jnp.exp2(s3 - m)
    return e, jnp.sum(e, axis=-1, keepdims=True)


def _online_update(m, l, acc, s3, pv):
    m_new = jnp.maximum(m, jnp.max(s3, axis=-1, keepdims=True))
    alpha = jnp.exp2(m - m_new)
    e = jnp.exp2(s3 - m_new)
    return m_new, alpha * l + jnp.sum(e, axis=-1, keepdims=True), alpha * acc + pv(e)


def _select_rows(imp, cur, n_sel):
    jidx = _iota(imp.shape, 1)
    valid = jidx <= cur
    forced = valid & ((jidx == 0) | (jidx == cur) | (jidx == cur - 1))
    sc = jnp.where(forced, FORCE, jnp.where(valid, imp, NEG))
    rank = jnp.zeros(imp.shape, jnp.int32)
    for i in range(n_sel):
        col = sc[:, i:i + 1]
        beats = (col > sc) | ((col == sc) & (jidx > i))
        rank = rank + jnp.where(beats, 1, 0)
    return (rank < min(N_SLC, n_sel)) & valid


def _select_cols(imp_t, cur, n_sel):
    jidx = _iota(imp_t.shape, 0)
    valid = jidx <= cur
    forced = valid & ((jidx == 0) | (jidx == cur) | (jidx == cur - 1))
    sc = jnp.where(forced, FORCE, jnp.where(valid, imp_t, NEG))
    rank = jnp.zeros(imp_t.shape, jnp.int32)
    for i in range(n_sel):
        row = sc[i:i + 1, :]
        beats = (row > sc) | ((row == sc) & (jidx > i))
        rank = rank + jnp.where(beats, 1, 0)
    return jnp.where((rank < min(N_SLC, n_sel)) & valid, 1.0, 0.0)


def _attn_prompt_kernel(q_ref, kc_ref, vct_ref, ksa_ref, vst_ref, kw_ref, vwt_ref, gates_ref, ovt_ref, o_ref):
    tq = q_ref.shape[1]
    nc = kc_ref.shape[1]
    chunk = vst_ref.shape[3]
    n_sel = ovt_ref.shape[0]
    cols = Q_PER_KV * tq
    groups = range(N_KV_HEADS)
    t0 = pl.program_id(1) * tq
    qpos = t0 + _iota((1, tq), 1)
    per_head = lambda a: jnp.concatenate([a] * Q_PER_KV, axis=1)
    q_t = q_ref[0].T
    zero = jnp.zeros((HEAD_DIM, tq), F32)
    qt = []
    for g in groups:
        blocks = []
        for r in range(Q_PER_KV):
            h = g * Q_PER_KV + r
            head = q_t[h * HEAD_DIM:(h + 1) * HEAD_DIM]
            blocks.append(jnp.concatenate([head, zero] if g == 0 else [zero, head], axis=0))
        qt.append(jnp.concatenate(blocks, axis=1).astype(BF16))
    row_lo = _iota((LANE, cols), 0) < HEAD_DIM
    row_half = _iota((LANE, 1), 0) // HEAD_DIM
    with_ones = lambda vt, g: jnp.where(row_half == g, vt, 1.0)

    def pair(a, normalised):
        o = jnp.where(row_lo, a[0], a[1])
        if normalised:
            return o
        return o / jnp.concatenate([a[0][HEAD_DIM:], a[1][:HEAD_DIM]], axis=0)

    cbias = per_head(jnp.where(_iota((nc, tq), 0) * CMP_STRIDE + CMP_BLOCK - 1 <= qpos, 0.0, NEG))
    cvalid = per_head(jnp.where(qpos >= CMP_BLOCK - 1, 1.0, 0.0))
    cur = qpos // SLC_BLOCK
    o_cmp, rhs = [], []
    for g in groups:
        s = _dot(kc_ref[0], qt[g]) + cbias
        e = jnp.exp2(s - jnp.max(s, axis=0, keepdims=True))
        p = e * (cvalid / jnp.sum(e, axis=0, keepdims=True))
        o_cmp.append(_dot(vct_ref[0], p.astype(BF16)))
        psum = p[:, 0:tq]
        for r in range(1, Q_PER_KV):
            psum = psum + p[:, r * tq:(r + 1) * tq]
        hi, lo = _split_hilo(psum)
        sel = _select_cols(_dot(ovt_ref[...], hi) + _dot(ovt_ref[...], lo), cur, n_sel)
        sel = jnp.concatenate([sel, jnp.zeros((LANE - n_sel, tq), F32)], axis=0)
        selneg = per_head(jnp.where(sel > 0.5, 0.0, NEG)).astype(BF16)
        rhs.append(jnp.concatenate([qt[g], selneg], axis=0))

    def slc_step(c, carry, causal):
        ka = ksa_ref[0, pl.ds(pl.multiple_of(c * chunk, chunk), chunk), :]
        vt = vst_ref[0, c]
        out = []
        for g in groups:
            m, acc = carry[g]
            s = _dot(ka, rhs[g])
            if causal is not None:
                s = jnp.where(causal, s, NEG)
            m_new = jnp.maximum(m, jnp.max(s, axis=0, keepdims=True))
            w = jnp.exp2(s - m_new).astype(BF16)
            out.append((m_new, jnp.exp2(m - m_new) * acc + _dot(with_ones(vt, g), w)))
        return tuple(out)

    init = tuple((jnp.full((1, cols), NEG, F32), jnp.zeros((LANE, cols), F32)) for g in groups)
    c_diag = t0 // chunk
    slc = lax.fori_loop(0, c_diag, lambda c, carry: slc_step(c, carry, None), init)
    slc = slc_step(c_diag, slc, per_head(c_diag * chunk + _iota((chunk, tq), 0) <= qpos))

    wtile0 = jnp.maximum(t0 - WINDOW, 0) // LANE
    wstart = pl.multiple_of(wtile0 * LANE, LANE)
    wn = WINDOW + tq
    wdiff = qpos - (wstart + _iota((wn, tq), 0))
    wbias = per_head(jnp.where((wdiff >= 0) & (wdiff < WINDOW), 0.0, NEG))
    kwin = kw_ref[0, pl.ds(wstart, wn), :]
    vwt = jnp.concatenate([vwt_ref[0, wtile0 + j] for j in range(wn // LANE)], axis=1)
    win = []
    for g in groups:
        s = _dot(kwin, qt[g]) + wbias
        w = jnp.exp2(s - jnp.max(s, axis=0, keepdims=True)).astype(BF16)
        win.append(_dot(with_ones(vwt, g), w))

    branches = (pair(o_cmp, True), pair([slc[g][1] for g in groups], False), pair(win, False))
    gt = gates_ref[0].T
    out = None
    for br in range(N_BRANCH):
        gate = jnp.concatenate(
            [jnp.concatenate([jnp.broadcast_to(gt[(g * Q_PER_KV + r) * N_BRANCH + br][None], (HEAD_DIM, tq))
                              for g in groups], axis=0) for r in range(Q_PER_KV)], axis=1)
        out = gate * branches[br] if out is None else out + gate * branches[br]
    heads = [out[g * HEAD_DIM:(g + 1) * HEAD_DIM, r * tq:(r + 1) * tq] for g in groups for r in range(Q_PER_KV)]
    o_ref[0] = jnp.concatenate(heads, axis=0).T.astype(BF16)


def _attn_prompt(q, kc, vct, ksa, vst, kw, vwt, gates, ovt):
    bx, t, _ = q.shape
    whole = lambda a: pl.BlockSpec((1,) + a.shape[1:], lambda b, i: (b,) + (0,) * (a.ndim - 1))
    const = lambda a: pl.BlockSpec(a.shape, lambda b, i: (0,) * a.ndim)
    return pl.pallas_call(
        _attn_prompt_kernel,
        grid=(bx, t // TQ),
        in_specs=[pl.BlockSpec((1, TQ, ATTN_W), lambda b, i: (b, i, 0)),
                  whole(kc), whole(vct), whole(ksa), whole(vst), whole(kw), whole(vwt),
                  pl.BlockSpec((1, TQ, GATE_PAD), lambda b, i: (b, i, 0)),
                  const(ovt)],
        out_specs=pl.BlockSpec((1, TQ, ATTN_W), lambda b, i: (b, i, 0)),
        out_shape=jax.ShapeDtypeStruct((bx, t, ATTN_W), BF16),
        compiler_params=_params("parallel", "parallel"),
        name="attn_prompt",
    )(q, kc, vct, ksa, vst, kw, vwt, gates, ovt)


def _attn_sample_a_kernel(q_ref, kc_ref, vc_ref, wst_ref, wnew_ref, ov_ref, e_ref, ocmp_ref, owin_ref, bias_ref,
                          *, pos0, n_cmp, n_sel):
    t = q_ref.shape[1]
    nc = kc_ref.shape[1]
    n_state = wst_ref.shape[3]
    nr = N_Q_HEADS
    lane = _iota((t, LANE), 1)
    tcol = _iota((t, 1), 0)
    qpos = pos0 + tcol
    qbd = jnp.concatenate(_q_rows(q_ref[0], lane), axis=0)
    cidx = _iota((t, nc), 1)
    cbias = jnp.where((cidx < n_cmp) & ((cidx * CMP_STRIDE + CMP_BLOCK - 1) <= qpos), 0.0, NEG)
    cvalid = jnp.where(qpos >= CMP_BLOCK - 1, 1.0, 0.0)
    e, l = _softmax_parts(_dot_nt(qbd, kc_ref[0]).reshape(nr, t, nc) + cbias[None])
    p = e * (cvalid / l)
    ocmp_ref[0] = _dot(p.reshape(nr * t, nc).astype(BF16), vc_ref[0])
    p4 = p.reshape(N_KV_HEADS, Q_PER_KV, t, nc)
    psum = (p4[:, 0] + p4[:, 1] + p4[:, 2] + p4[:, 3]).reshape(N_KV_HEADS * t, nc)
    imp = _dot_hilo(psum, ov_ref[...])
    cur = jnp.concatenate([qpos // SLC_BLOCK] * N_KV_HEADS, axis=0)
    selneg = jnp.where(_select_rows(imp, cur, n_sel), 0.0, NEG).astype(BF16)
    bias_ref[0] = _dot(selneg, e_ref[...]).reshape(N_KV_HEADS, t, e_ref.shape[1])
    wst = wst_ref[0, 0]
    kwt = wst[0:KV_W].astype(BF16)
    vwt = wst[KV_W:2 * KV_W].astype(BF16)
    wnew = _pad_rows(wnew_ref[0], NEW_PAD).astype(BF16)
    sdiff = qpos - (pos0 - n_state + _iota((t, n_state), 1))
    sbias = jnp.where((sdiff >= 0) & (sdiff < WINDOW), 0.0, NEG)
    nbias = jnp.where(tcol - _iota((t, NEW_PAD), 1) >= 0, 0.0, NEG)
    s_a = _dot(qbd, kwt).reshape(nr, t, n_state) + sbias[None]
    s_b = _dot_nt(qbd, wnew[:, 0:KV_W]).reshape(nr, t, NEW_PAD) + nbias[None]
    m = jnp.maximum(jnp.max(s_a, axis=-1, keepdims=True), jnp.max(s_b, axis=-1, keepdims=True))
    e_a = jnp.exp2(s_a - m)
    e_b = jnp.exp2(s_b - m)
    l = jnp.sum(e_a, axis=-1, keepdims=True) + jnp.sum(e_b, axis=-1, keepdims=True)
    o = (_dot_nt(e_a.reshape(nr * t, n_state).astype(BF16), vwt)
         + _dot(e_b.reshape(nr * t, NEW_PAD).astype(BF16), wnew[:, KV_W:2 * KV_W]))
    owin_ref[0] = o / l.reshape(nr * t, 1)


def _attn_sample_a(q, kc, vc, win_state_t, layer, win_new, ov, expand, pos0, n_cmp, n_sel):
    bx, t, _ = q.shape
    nc = kc.shape[1]
    n_state = win_state_t.shape[3]
    kp = expand.shape[1]
    rows = N_Q_HEADS * t
    per_b = lambda a: pl.BlockSpec((1,) + a.shape[1:], lambda b: (b,) + (0,) * (a.ndim - 1))
    const = lambda a: pl.BlockSpec(a.shape, lambda b: (0,) * a.ndim)
    return pl.pallas_call(
        functools.partial(_attn_sample_a_kernel, pos0=pos0, n_cmp=n_cmp, n_sel=n_sel),
        grid=(bx,),
        in_specs=[per_b(q), per_b(kc), per_b(vc),
                  pl.BlockSpec((1, 1, 2 * KV_W, n_state), lambda b: (layer, b, 0, 0)),
                  per_b(win_new), const(ov), const(expand)],
        out_specs=[pl.BlockSpec((1, rows, LANE), lambda b: (b, 0, 0)),
                   pl.BlockSpec((1, rows, LANE), lambda b: (b, 0, 0)),
                   pl.BlockSpec((1, N_KV_HEADS, t, kp), lambda b: (b, 0, 0, 0))],
        out_shape=[jax.ShapeDtypeStruct((bx, rows, LANE), F32),
                   jax.ShapeDtypeStruct((bx, rows, LANE), F32),
                   jax.ShapeDtypeStruct((bx, N_KV_HEADS, t, kp), F32)],
        compiler_params=_params("parallel"),
        name="attn_sample_a",
    )(q, kc, vc, win_state_t, win_new, ov, expand)


def _attn_sample_b_kernel(pt_ref, *refs, pps):
    pages = refs[:pps]
    (q_ref, kvn_ref, bias_ref, ocmp_ref, owin_ref, gates_ref, o_ref, s_ref, vt_ref) = refs[pps:]
    t = q_ref.shape[1]
    page = pages[0].shape[3]
    n = pps * page
    nr = N_Q_HEADS
    n_steps = s_ref.shape[0]
    step = pl.program_id(1)
    lane = _iota((t, LANE), 1)
    qbd = jnp.concatenate(_q_rows(q_ref[0], lane), axis=0)
    kt = jnp.concatenate([pages[k][0, 0, 0:KV_W, :].astype(BF16) for k in range(pps)], axis=1)
    s_ref[step] = _dot(qbd, kt)
    for k in range(pps):
        vt_ref[step, :, k * page:(k + 1) * page] = pages[k][0, 0, KV_W:2 * KV_W, :].astype(BF16)

    @pl.when(step == n_steps - 1)
    def _():
        biased = lambda s_flat, bias: (s_flat.reshape(N_KV_HEADS, Q_PER_KV, t, s_flat.shape[-1])
                                       + bias[:, None]).reshape(nr, t, s_flat.shape[-1])
        parts = [biased(s_ref[j], bias_ref[0, :, :, j * n:(j + 1) * n]) for j in range(n_steps)]
        kvn = _pad_rows(kvn_ref[0], NEW_PAD).astype(BF16)
        causal = _iota((t, NEW_PAD), 1) <= _iota((t, 1), 0)
        bias_new = jnp.where(causal[None], bias_ref[0, :, :, n_steps * n:n_steps * n + NEW_PAD], NEG)
        s_new = biased(_dot_nt(qbd, kvn[:, 0:KV_W]), bias_new)
        m = jnp.max(s_new, axis=-1, keepdims=True)
        for p in parts:
            m = jnp.maximum(m, jnp.max(p, axis=-1, keepdims=True))
        e = jnp.exp2(s_new - m)
        l = jnp.sum(e, axis=-1, keepdims=True)
        acc = _dot(e.reshape(nr * t, NEW_PAD).astype(BF16), kvn[:, KV_W:2 * KV_W])
        for j, p in enumerate(parts):
            e = jnp.exp2(p - m)
            l = l + jnp.sum(e, axis=-1, keepdims=True)
            acc = acc + _dot_nt(e.reshape(nr * t, n).astype(BF16), vt_ref[j])
        o_slc = acc.reshape(nr, t, LANE) / l
        gates = gates_ref[0]
        vals = [_gate_combine(gates, h, ocmp_ref[0, h * t:(h + 1) * t], o_slc[h], owin_ref[0, h * t:(h + 1) * t])
                for h in range(nr)]
        o_ref[0] = _assemble_heads(vals, lane)


def _attn_sample_b(cache_t, layer, page_table, q, kv_new, bias, o_cmp, o_win, gates, pps):
    bx, n_pages = page_table.shape
    page = cache_t.shape[3]
    t = q.shape[1]
    rows = N_Q_HEADS * t
    n = pps * page
    page_specs = [pl.BlockSpec((1, 1, 2 * KV_W, page),
                               lambda b, s, pt, k=k: (layer, pt[b, s * pps + k], 1, 0))
                  for k in range(pps)]
    return pl.pallas_call(
        functools.partial(_attn_sample_b_kernel, pps=pps),
        grid_spec=pltpu.PrefetchScalarGridSpec(
            num_scalar_prefetch=1,
            grid=(bx, n_pages // pps),
            in_specs=page_specs + [
                pl.BlockSpec((1, t, ATTN_W), lambda b, s, pt: (b, 0, 0)),
                pl.BlockSpec((1, t, 2 * KV_W), lambda b, s, pt: (b, 0, 1)),
                pl.BlockSpec((1,) + bias.shape[1:], lambda b, s, pt: (b, 0, 0, 0)),
                pl.BlockSpec((1, rows, LANE), lambda b, s, pt: (b, 0, 0)),
                pl.BlockSpec((1, rows, LANE), lambda b, s, pt: (b, 0, 0)),
                pl.BlockSpec((1, t, GATE_PAD), lambda b, s, pt: (b, 0, 0))],
            out_specs=pl.BlockSpec((1, t, ATTN_W), lambda b, s, pt: (b, 0, 0)),
            scratch_shapes=[pltpu.VMEM((n_pages // pps, rows, n), F32),
                            pltpu.VMEM((n_pages // pps, KV_W, n), BF16)]),
        out_shape=jax.ShapeDtypeStruct((bx, t, ATTN_W), F32),
        compiler_params=_params("parallel", "arbitrary"),
        name="attn_sample_b",
    )(page_table, *([cache_t] * pps), q, kv_new, bias, o_cmp, o_win, gates)


def _mix_out_kernel(x_ref, mod_ref, oattn_ref, bg_ref, u_ref, uh_ref, p_ref, ph_ref, cw_ref, cb_ref,
                    pw_ref, ps_ref, wo_ref, norm_ref, x1_ref, h2_ref, *, pos0, zero_first_halo):
    bb, tt, d = x_ref.shape
    rows = bb * tt
    i = pl.program_id(1)
    uh = uh_ref[...]
    ph = ph_ref[...]
    if zero_first_halo:
        keep = jnp.where(i > 0, 1.0, 0.0)
        uh = uh * keep
        ph = ph * keep
    ucat = jnp.concatenate([uh, u_ref[...]], axis=1)
    conv = cb_ref[...]
    for j in range(CONV_K):
        off = CONV_HALO - CONV_BUF + j
        conv = conv + ucat[:, off:off + tt] * cw_ref[j:j + 1]
    y_conv = bg_ref[...] * conv
    pcat = jnp.concatenate([ph, p_ref[...]], axis=1)
    a2 = pcat[:, 1:] + pcat[:, :-1]
    a4 = a2[:, 2:] + a2[:, :-2]
    a8 = a4[:, 4:] + a4[:, :-4]
    a16 = a8[:, 8:] + a8[:, :-8]
    sums = (a2[:, POOL_HALO - 1:POOL_HALO - 1 + tt], a4[:, POOL_HALO - 3:POOL_HALO - 3 + tt],
            a8[:, POOL_HALO - 7:POOL_HALO - 7 + tt], a16[:, POOL_HALO - 15:POOL_HALO - 15 + tt])
    pos = pos0 + i * tt + _iota((1, tt, 1), 1)
    grp = _iota((1, 1, POOL_W), 2) // (POOL_W // len(POOL_WINDOWS))
    mean = jnp.zeros((bb, tt, POOL_W), F32)
    for gi, w in enumerate(POOL_WINDOWS):
        cnt = jnp.minimum(w, pos + 1).astype(F32)
        mean = jnp.where(grp == gi, sums[gi] / cnt, mean)
    dlt = (mean - p_ref[...]).reshape(rows, POOL_W).astype(BF16)
    y_pool = _dot(dlt, pw_ref[...]) * ps_ref[...]
    mix = _dot(oattn_ref[...].reshape(rows, ATTN_W).astype(BF16), wo_ref[0, 0:ATTN_W])
    mix = mix + _dot(y_conv.reshape(rows, CONV_W).astype(BF16), wo_ref[0, ATTN_W:ATTN_W + CONV_W])
    mix = mix + _dot(y_pool.astype(BF16), wo_ref[0, ATTN_W + CONV_W:ATTN_W + CONV_W + POOL_W])
    gate1 = mod_ref[:, :, 2 * d:3 * d]
    x1 = x_ref[...] + gate1 * mix.reshape(bb, tt, d)
    x1_ref[...] = x1
    y = x1 * lax.rsqrt(jnp.mean(x1 * x1, axis=-1, keepdims=True) + EPS) * norm_ref[...]
    h2_ref[...] = (y * (1.0 + mod_ref[:, :, 4 * d:5 * d]) + mod_ref[:, :, 3 * d:4 * d]).astype(h2_ref.dtype)


def _mix_out(x, mod, oattn, bg, u, u_halo, pool, p_halo, conv_w, conv_b, pool_wbd, pool_scale, w_out, layer, norm2,
             bb, tt, pos0, halo_from_self):
    bx, t, d = x.shape
    wo_spec = pl.BlockSpec((1,) + w_out.shape[1:], lambda b, i: (layer, 0, 0), pipeline_mode=pl.Buffered(1))
    blk = lambda w: pl.BlockSpec((bb, tt, w), lambda b, i: (b, i, 0))
    full = lambda a: pl.BlockSpec(a.shape, lambda b, i: (0,) * a.ndim)
    if halo_from_self:
        uh_spec = pl.BlockSpec((bb, CONV_HALO, CONV_W),
                               lambda b, i: (b, jnp.maximum(i * (tt // CONV_HALO) - 1, 0), 0))
        ph_spec = pl.BlockSpec((bb, POOL_HALO, POOL_W),
                               lambda b, i: (b, jnp.maximum(i * (tt // POOL_HALO) - 1, 0), 0))
    else:
        uh_spec = pl.BlockSpec((bb, CONV_HALO, CONV_W), lambda b, i: (b, 0, 0))
        ph_spec = pl.BlockSpec((bb, POOL_HALO, POOL_W), lambda b, i: (b, 0, 0))
    return pl.pallas_call(
        functools.partial(_mix_out_kernel, pos0=pos0, zero_first_halo=halo_from_self),
        grid=(bx // bb, t // tt),
        in_specs=[blk(d), pl.BlockSpec((bb, 1, mod.shape[-1]), lambda b, i: (b, 0, 0)),
                  blk(ATTN_W), blk(CONV_W), blk(CONV_W), uh_spec, blk(POOL_W), ph_spec,
                  full(conv_w), full(conv_b), full(pool_wbd), full(pool_scale), wo_spec, full(norm2)],
        out_specs=[blk(d), blk(d)],
        out_shape=[jax.ShapeDtypeStruct((bx, t, d), F32), jax.ShapeDtypeStruct((bx, t, d), oattn.dtype)],
        compiler_params=_params("parallel", "parallel"),
        name="mix_out",
    )(x, mod, oattn, bg, u, u_halo, pool, p_halo, conv_w, conv_b, pool_wbd, pool_scale, w_out, norm2)


def _ffn_kernel(x1_ref, h2_ref, mod_ref, wa_ref, wb_ref, wd_ref, o_ref):
    bb, tt, d = x1_ref.shape
    h = h2_ref[...].reshape(bb * tt, d).astype(BF16)
    a = _dot(h, wa_ref[0])
    b = _dot(h, wb_ref[0])
    y = _dot((_silu(a) * b).astype(BF16), wd_ref[0])
    o_ref[...] = x1_ref[...] + mod_ref[:, :, 5 * d:6 * d] * y.reshape(bb, tt, d)


def _ffn(x1, h2, mod, w_up, w_down, layer, bb, tt):
    bx, t, d = x1.shape
    hid = w_down.shape[1]
    blk = pl.BlockSpec((bb, tt, d), lambda b, i: (b, i, 0))
    resident = lambda shape, imap: pl.BlockSpec(shape, imap, pipeline_mode=pl.Buffered(1))
    return pl.pallas_call(
        _ffn_kernel,
        grid=(bx // bb, t // tt),
        in_specs=[blk, blk, pl.BlockSpec((bb, 1, mod.shape[-1]), lambda b, i: (b, 0, 0)),
                  resident((1, d, hid), lambda b, i: (layer, 0, 0)),
                  resident((1, d, hid), lambda b, i: (layer, 0, 1)),
                  resident((1, hid, d), lambda b, i: (layer, 0, 0))],
        out_specs=blk,
        out_shape=jax.ShapeDtypeStruct((bx, t, d), F32),
        compiler_params=_params("parallel", "parallel"),
        name="ffn",
    )(x1, h2, mod, w_up, w_up, w_down)


def _overlap_matrix(n_cmp, n_sel, rows, lanes):
    m = np.zeros((rows, lanes), np.float32)
    i = np.arange(n_cmp)
    for part in range(CMP_BLOCK // CMP_STRIDE):
        j = np.minimum((i + part) * CMP_STRIDE // SLC_BLOCK, n_sel - 1)
        np.add.at(m, (i, j), 1.0)
    return m


def _expand_matrix(n_blocks, n_keys):
    return (np.arange(n_blocks)[:, None] == np.arange(n_keys)[None, :] // SLC_BLOCK).astype(np.float32)


def _chunk_order_perm(page):
    pos = np.arange(page)
    m = np.zeros((page, page), np.float32)
    m[(pos % CMP_STRIDE) * (page // CMP_STRIDE) + pos // CMP_STRIDE, pos] = 1.0
    return jnp.asarray(np.stack([m, m.T]), BF16)


def _seg_matrix(n):
    idx = np.arange(n) // HEAD_DIM
    return jnp.asarray((idx[:, None] == idx[None, :]).astype(np.float32) / HEAD_DIM, BF16)


def _prep_weights(w_in, w_out, q_norm, k_norm, cmp_pe, cmp_w1, cmp_w2, pool_w, w_up, w_down):
    depth, d, _ = w_in.shape
    g0 = ATTN_W + 6 * KV_W
    w_in_r = jnp.concatenate([w_in[:, :, :g0], w_in[:, :, g0 + N_Q_HEADS * N_BRANCH:],
                              w_in[:, :, g0:g0 + N_Q_HEADS * N_BRANCH],
                              jnp.zeros((depth, d, GATE_PAD - N_Q_HEADS * N_BRANCH), w_in.dtype)], axis=-1)
    eye_g = jnp.eye(N_KV_HEADS, dtype=BF16)
    w1r = cmp_w1.astype(BF16).reshape(depth, 2, 2, CMP_STRIDE, HEAD_DIM, HEAD_DIM)
    w1t = jnp.transpose(w1r, (0, 1, 3, 4, 2, 5))
    wr = w1t[:, :, :, None, :, :, None, :] * eye_g[None, None, None, :, None, None, :, None]
    wr = wr.reshape(depth, 2, CMP_STRIDE * KV_W, 2 * KV_W)
    w2bd = (cmp_w2.astype(BF16)[:, :, None, :, None, :]
            * eye_g[None, None, :, None, :, None]).reshape(depth, 2, KV_W, KV_W)
    pe = cmp_pe.reshape(depth, 2, 2, CMP_STRIDE, 1, HEAD_DIM)
    pe = jnp.broadcast_to(pe, (depth, 2, 2, CMP_STRIDE, N_KV_HEADS, HEAD_DIM))
    pe = pe.reshape(depth, 2, 2, CMP_STRIDE * KV_W)
    pe_aug = jnp.concatenate([pe, jnp.zeros((depth, 2, 6, CMP_STRIDE * KV_W), F32)], axis=2)
    eye_p = jnp.eye(len(POOL_WINDOWS), dtype=F32)
    pool_bd = (pool_w[:, :, :, None, :] * eye_p[None, :, None, :, None]).reshape(depth, POOL_W, POOL_W)
    return dict(
        w_in=w_in_r.astype(BF16), w_out=w_out.astype(BF16), w_up=w_up.astype(BF16), w_down=w_down.astype(BF16),
        wr=wr.astype(BF16), pe_aug=pe_aug, w2bd=w2bd.astype(BF16), pool_bd=pool_bd.astype(BF16),
        q_gain=jnp.tile(q_norm, (1, N_Q_HEADS))[:, None, :],
        k_gain=jnp.tile(k_norm, (1, 1, N_KV_HEADS)),
    )


def _feature_major_view(a):
    lead = a.shape[:-2]
    f, p = a.shape[-2:]
    a = a.reshape(lead + (f // KV_W, N_KV_HEADS, HEAD_DIM, p))
    n = len(lead)
    return jnp.transpose(a, tuple(range(n)) + (n + 3, n, n + 1, n + 2))


def _layer_prompt(x, mod, l, W, P, C):
    bx, t, d = x.shape
    bb, tt = 1, ROW_TILE
    kg = P["k_gain"][l]
    q, bg, u, pool, gates, kv_t, win_t, raw, ksa, vst, kw, vwt = _proj_in(
        x, mod, W["norm_mix"][l][None, None], P["w_in"], l, P["q_gain"][l], kg[1:3],
        C["seg_q"], C["seg_k"], bb, tt, True)
    kc, vct = _cmp_prompt(raw, P["wr"][l], P["pe_aug"][l], P["w2bd"][l], C["seg_k"], kg[0:1])
    oattn = _attn_prompt(q, kc, vct, ksa, vst, kw, vwt, gates, C["ovt_prompt"])
    x1, h2 = _mix_out(x, mod, oattn, bg, u, u, pool, pool, W["conv_w"][l], W["conv_bias"][l][None],
                      P["pool_bd"][l], W["pool_scale"][l][None], P["w_out"], l, W["norm_ffn"][l][None, None],
                      bb, tt, 0, True)
    x2 = _ffn(x1, h2, mod, P["w_up"], P["w_down"], l, bb, tt)
    states = (kv_t, win_t[:, :, t - min(WINDOW, t):], u[:, t - CONV_BUF:], pool[:, t - POOL_BUF:])
    return x2, states


def _layer_sample(x, mod, l, W, P, C, cache_t, page_table, win_state_t, conv_state, pool_state):
    bx, t, d = x.shape
    n_pages = page_table.shape[1]
    pos0 = n_pages * cache_t.shape[3]
    pps = C["pps"]
    kg = P["k_gain"][l]
    q, bg, u, pool, gates, kv, win = _proj_in(
        x, mod, W["norm_mix"][l][None, None], P["w_in"], l, P["q_gain"][l], kg[1:3],
        C["seg_q"], C["seg_k"], bx, t, False)
    kc, vc = _cmp_sample(cache_t, l, page_table, C["perm"], P["wr"][l], P["pe_aug"][l], P["w2bd"][l],
                         C["seg_k"], kg[0:1], pps)
    o_cmp, o_win, bias = _attn_sample_a(q, kc, vc, win_state_t, l, win, C["ov_sample"], C["expand_sample"],
                                        pos0, C["n_cmp_s"], C["n_sel_s"])
    oattn = _attn_sample_b(cache_t, l, page_table, q, kv, bias, o_cmp, o_win, gates, pps)
    u_halo = jnp.concatenate([jnp.zeros((bx, CONV_HALO - CONV_BUF, CONV_W), F32), conv_state], axis=1)
    p_halo = jnp.concatenate([jnp.zeros((bx, POOL_HALO - POOL_BUF, POOL_W), F32), pool_state], axis=1)
    x1, h2 = _mix_out(x, mod, oattn, bg, u, u_halo, pool, p_halo, W["conv_w"][l], W["conv_bias"][l][None],
                      P["pool_bd"][l], W["pool_scale"][l][None], P["w_out"], l, W["norm_ffn"][l][None, None],
                      bx, t, pos0, False)
    x2 = _ffn(x1, h2, mod, P["w_up"], P["w_down"], l, bx, t)
    win_t = jnp.concatenate([win_state_t[l][:, :, t:], jnp.swapaxes(win, 1, 2)], axis=2)
    states = (kv.reshape(bx, t, 4, N_KV_HEADS, HEAD_DIM), win_t,
              jnp.concatenate([conv_state, u], axis=1)[:, t:],
              jnp.concatenate([pool_state, pool], axis=1)[:, t:])
    return x2, states


def kernel(x_prompt, x_sample, cache_nsa_kv, state_win_kv, state_conv, state_pool, page_table, c_prompt, c_sample, norm_mix, norm_ffn, w_ada, b_ada, w_in, w_out, q_norm, k_norm, cmp_pe, cmp_w1, cmp_w2, conv_w, conv_bias, pool_w, pool_scale, w_up, w_down):
    depth = w_in.shape[0]
    bp, tp, d = x_prompt.shape
    bs, ts, _ = x_sample.shape
    n_pages = page_table.shape[1]
    n_pool, page = cache_nsa_kv.shape[1:3]
    past = n_pages * page
    n_state = state_win_kv.shape[2]
    hid = w_down.shape[1]
    assert tp % ROW_TILE == 0 and tp >= WINDOW + TQ and past % SLC_BLOCK == 0 and ts <= CONV_HALO

    W = dict(norm_mix=norm_mix, norm_ffn=norm_ffn, conv_w=conv_w, conv_bias=conv_bias, pool_scale=pool_scale)
    P = _prep_weights(w_in, w_out, q_norm, k_norm, cmp_pe, cmp_w1, cmp_w2, pool_w, w_up, w_down)
    n_cmp_p = (tp - CMP_BLOCK) // CMP_STRIDE + 1
    n_sel_p = tp // SLC_BLOCK
    n_cmp_s = (past + ts - CMP_BLOCK) // CMP_STRIDE + 1
    n_sel_s = -(-(past + ts) // SLC_BLOCK)
    sel_lanes = -(-n_sel_s // LANE) * LANE
    pps = min(PAGES_PER_STEP, n_pages)
    C = dict(
        seg_q=_seg_matrix(ATTN_W), seg_k=_seg_matrix(KV_W), perm=_chunk_order_perm(page),
        ovt_prompt=jnp.asarray(_overlap_matrix(n_cmp_p, n_sel_p, tp // CMP_STRIDE, n_sel_p).T, BF16),
        ov_sample=jnp.asarray(_overlap_matrix(n_cmp_s, n_sel_s, past // CMP_STRIDE, sel_lanes), BF16),
        expand_sample=jnp.asarray(_expand_matrix(sel_lanes, past + NEW_PAD), BF16),
        n_cmp_s=n_cmp_s, n_sel_s=n_sel_s, pps=pps,
    )
    cache_t = jnp.transpose(cache_nsa_kv, (0, 1, 3, 4, 5, 2)).reshape(depth, n_pool, 4 * KV_W, page)
    win_state_t = jnp.transpose(state_win_kv, (0, 1, 3, 4, 5, 2)).reshape(depth, bs, 2 * KV_W, n_state)
    mod = _ada(jnp.concatenate([c_prompt, c_sample], axis=0), w_ada, b_ada)
    mod_p = mod[:, :bp, None, :]
    mod_s = mod[:, bp:, None, :]

    yp, ys = x_prompt, x_sample
    outs_p, outs_s = [], []
    for l in range(depth):
        yp, st = _layer_prompt(yp, mod_p[l], l, W, P, C)
        outs_p.append(st)
        ys, st = _layer_sample(ys, mod_s[l], l, W, P, C, cache_t, page_table, win_state_t,
                               state_conv[l], state_pool[l])
        outs_s.append(st)
    stack = lambda outs, k: jnp.stack([o[k] for o in outs])
    return (yp, ys, _feature_major_view(stack(outs_p, 0)), stack(outs_s, 0),
            _feature_major_view(stack(outs_p, 1)), _feature_major_view(stack(outs_s, 1)),
            stack(outs_p, 2), stack(outs_s, 2), stack(outs_p, 3), stack(outs_s, 3))
```

```python
import functools

import numpy as np
import jax
import jax.numpy as jnp
from jax import lax
from jax.experimental import pallas as pl
from jax.experimental.pallas import tpu as pltpu

F32 = jnp.float32
BF16 = jnp.bfloat16

HEAD_DIM = 64
N_Q_HEADS = 8
N_KV_HEADS = 2
Q_PER_KV = N_Q_HEADS // N_KV_HEADS
N_BRANCH = 3
ATTN_W = N_Q_HEADS * HEAD_DIM
KV_W = N_KV_HEADS * HEAD_DIM
CONV_W = 256
POOL_W = 256
CONV_K = 3
CONV_BUF = CONV_K - 1
POOL_WINDOWS = (2, 4, 8, 16)
POOL_BUF = max(POOL_WINDOWS) - 1
CMP_BLOCK = 32
CMP_STRIDE = 16
SLC_BLOCK = 64
N_SLC = 16
WINDOW = 512
EPS = 1e-6
NEG = -1e30
FORCE = 1e4
ATTN_SCALE = HEAD_DIM ** -0.5
LOG2E = 1.4426950408889634
LANE = 128
GATE_PAD = LANE
IN_W_PAD = ATTN_W + 4 * KV_W + 2 * KV_W + 3 * CONV_W + POOL_W + GATE_PAD
CONV_HALO = 8
POOL_HALO = 16
NEW_PAD = LANE
VMEM_LIMIT = 56 * 1024 * 1024
ROW_TILE = 512
TQ = 512
SLC_CHUNK = 512
PAGES_PER_STEP = 64
ADA_TN = 1536

_NT = (((1,), (1,)), ((), ()))


def _params(*sem):
    return pltpu.CompilerParams(dimension_semantics=sem, vmem_limit_bytes=VMEM_LIMIT)


def _dot(a, b):
    return jnp.dot(a, b, preferred_element_type=F32)


def _dot_nt(a, b):
    return lax.dot_general(a, b, _NT, preferred_element_type=F32)


def _split_hilo(x):
    hi = x.astype(BF16)
    return hi, (x - hi.astype(F32)).astype(BF16)


def _dot_hilo(x, w_bf):
    hi, lo = _split_hilo(x)
    return _dot(hi, w_bf) + _dot(lo, w_bf)


def _seg_rms(x, seg_bf, gain):
    ms = _dot_hilo(x * x, seg_bf)
    return x * lax.rsqrt(ms + EPS) * gain


def _silu(x):
    return x * jax.nn.sigmoid(x)


def _pad_rows(x, n):
    return jnp.concatenate([x, jnp.zeros((n - x.shape[0], x.shape[1]), x.dtype)], axis=0)


def _iota(shape, axis):
    return lax.broadcasted_iota(jnp.int32, shape, axis)


def _ada_kernel(c_ref, w_ref, b_ref, o_ref):
    c = c_ref[...]
    o_ref[0] = _dot(_silu(c).astype(BF16), w_ref[0].astype(BF16)) + b_ref[0]


def _ada(c_all, w_ada, b_ada):
    depth, d, n = w_ada.shape
    r = c_all.shape[0]
    return pl.pallas_call(
        _ada_kernel,
        grid=(depth, n // ADA_TN),
        in_specs=[pl.BlockSpec((r, d), lambda l, j: (0, 0)),
                  pl.BlockSpec((1, d, ADA_TN), lambda l, j: (l, 0, j)),
                  pl.BlockSpec((1, 1, ADA_TN), lambda l, j: (l, 0, j))],
        out_specs=pl.BlockSpec((1, r, ADA_TN), lambda l, j: (l, 0, j)),
        out_shape=jax.ShapeDtypeStruct((depth, r, n), F32),
        compiler_params=_params("parallel", "parallel"),
        name="ada",
    )(c_all, w_ada, b_ada.reshape(depth, 1, n))


def _proj_in_kernel(x_ref, mod_ref, norm_ref, w_ref, qg_ref, kg_ref, seg_q_ref, seg_k_ref,
                    q_ref, bg_ref, u_ref, pool_ref, gates_ref, *kv_refs, feature_major):
    bb, tt, d = x_ref.shape
    rows = bb * tt
    x = x_ref[...]
    y = x * lax.rsqrt(jnp.mean(x * x, axis=-1, keepdims=True) + EPS) * norm_ref[...]
    shift1 = mod_ref[:, :, 0:d]
    scale1 = mod_ref[:, :, d:2 * d]
    h = (y * (1.0 + scale1) + shift1).reshape(rows, d).astype(BF16)
    z = _dot(h, w_ref[0])

    qn = _seg_rms(z[:, 0:ATTN_W], seg_q_ref[...], qg_ref[...]) * (ATTN_SCALE * LOG2E)
    q_ref[...] = qn.reshape(bb, tt, ATTN_W)
    o = ATTN_W
    raw_cmp = z[:, o:o + 2 * KV_W]
    ks = _seg_rms(z[:, o + 2 * KV_W:o + 3 * KV_W], seg_k_ref[...], kg_ref[0:1])
    vs = z[:, o + 3 * KV_W:o + 4 * KV_W]
    kv = jnp.concatenate([raw_cmp, ks, vs], axis=-1)
    o += 4 * KV_W
    kw = _seg_rms(z[:, o:o + KV_W], seg_k_ref[...], kg_ref[1:2])
    vw = z[:, o + KV_W:o + 2 * KV_W]
    win = jnp.concatenate([kw, vw], axis=-1)
    o += 2 * KV_W
    if feature_major:
        kvt_ref, wint_ref, raw_ref, ksa_ref, vst_ref, kw_ref, vwt_ref = kv_refs
        kvt = kv.T
        wint = win.T
        kvt_ref[0] = kvt
        wint_ref[0] = wint
        raw_ref[0] = raw_cmp
        key_block = (pl.program_id(1) * rows + _iota((rows, LANE), 0)) // SLC_BLOCK
        onehot = jnp.where(key_block == _iota((rows, LANE), 1), 1.0, 0.0)
        ksa_ref[0] = jnp.concatenate([ks, onehot], axis=-1).astype(BF16)
        vst = kvt[3 * KV_W:4 * KV_W].astype(BF16)
        for j in range(rows // SLC_CHUNK):
            vst_ref[0, j] = vst[:, j * SLC_CHUNK:(j + 1) * SLC_CHUNK]
        kw_ref[0] = kw.astype(BF16)
        vwt = wint[KV_W:2 * KV_W].astype(BF16)
        for j in range(rows // LANE):
            vwt_ref[0, j] = vwt[:, j * LANE:(j + 1) * LANE]
    else:
        kv_ref, win_ref = kv_refs
        kv_ref[...] = kv.reshape(bb, tt, 4 * KV_W)
        win_ref[...] = win.reshape(bb, tt, 2 * KV_W)
    bg_ref[...] = z[:, o:o + CONV_W].reshape(bb, tt, CONV_W)
    u_ref[...] = (z[:, o + CONV_W:o + 2 * CONV_W] * z[:, o + 2 * CONV_W:o + 3 * CONV_W]).reshape(bb, tt, CONV_W)
    o += 3 * CONV_W
    pool_ref[...] = z[:, o:o + POOL_W].reshape(bb, tt, POOL_W)
    o += POOL_W
    gates_ref[...] = jax.nn.sigmoid(z[:, o:o + GATE_PAD]).reshape(bb, tt, GATE_PAD)


def _proj_in(x, mod, norm1, w_in, layer, q_gain, k_gain2, seg_q, seg_k, bb, tt, feature_major):
    bx, t, d = x.shape
    w_spec = pl.BlockSpec((1,) + w_in.shape[1:], lambda b, i: (layer, 0, 0), pipeline_mode=pl.Buffered(1))
    grid = (bx // bb, t // tt)
    blk = lambda w: pl.BlockSpec((bb, tt, w), lambda b, i: (b, i, 0))
    full = lambda a: pl.BlockSpec(a.shape, lambda b, i: (0,) * a.ndim)
    outs = [((bx, t, ATTN_W), F32), ((bx, t, CONV_W), F32), ((bx, t, CONV_W), F32), ((bx, t, POOL_W), F32),
            ((bx, t, GATE_PAD), F32)]
    out_specs = [blk(s[-1]) for s, _ in outs]
    if feature_major:
        assert bb == 1 and tt % SLC_CHUNK == 0 and SLC_CHUNK % LANE == 0
        assert tt == min(WINDOW, t)
        outs += [((bx, 4 * KV_W, t), F32), ((bx, 2 * KV_W, tt), F32), ((bx, t, 2 * KV_W), F32),
                 ((bx, t, 2 * KV_W), BF16), ((bx, t // SLC_CHUNK, KV_W, SLC_CHUNK), BF16),
                 ((bx, t, KV_W), BF16), ((bx, t // LANE, KV_W, LANE), BF16)]
        out_specs += [pl.BlockSpec((1, 4 * KV_W, tt), lambda b, i: (b, 0, i)),
                      pl.BlockSpec((1, 2 * KV_W, tt), lambda b, i: (b, 0, 0)),
                      blk(2 * KV_W),
                      blk(2 * KV_W),
                      pl.BlockSpec((1, tt // SLC_CHUNK, KV_W, SLC_CHUNK), lambda b, i: (b, i, 0, 0)),
                      blk(KV_W),
                      pl.BlockSpec((1, tt // LANE, KV_W, LANE), lambda b, i: (b, i, 0, 0))]
    else:
        outs += [((bx, t, 4 * KV_W), F32), ((bx, t, 2 * KV_W), F32)]
        out_specs += [blk(4 * KV_W), blk(2 * KV_W)]
    return pl.pallas_call(
        functools.partial(_proj_in_kernel, feature_major=feature_major),
        grid=grid,
        in_specs=[blk(d), pl.BlockSpec((bb, 1, mod.shape[-1]), lambda b, i: (b, 0, 0)),
                  full(norm1), w_spec, full(q_gain), full(k_gain2), full(seg_q), full(seg_k)],
        out_specs=out_specs,
        out_shape=[jax.ShapeDtypeStruct(s, dt) for s, dt in outs],
        compiler_params=_params("parallel", "arbitrary" if feature_major else "parallel"),
        name="proj_in",
    )(x, mod, norm1, w_in, q_gain, k_gain2, seg_q, seg_k)


def _compress_core(chunks, nc, wr_ref, pe_ref, w2_ref, seg_k_ref, kg0_ref):
    accs = [_dot(chunks(s).astype(BF16), wr_ref[s]) for s in range(2)]
    outs = []
    for s in range(2):
        acc = accs[s]
        acc_pe = _dot(pe_ref[s].astype(BF16), wr_ref[s])
        bias = acc_pe[0:1, 0:KV_W] + acc_pe[1:2, KV_W:2 * KV_W]
        nxt = pltpu.roll(acc[:, KV_W:2 * KV_W], nc - 1, 0)
        pre = acc[:, 0:KV_W] + nxt + bias
        outs.append(_dot(_silu(pre).astype(BF16), w2_ref[s]))
    return _seg_rms(outs[0], seg_k_ref[...], kg0_ref[...]), outs[1]


def _cmp_prompt_kernel(k_ref, v_ref, wr_ref, pe_ref, w2_ref, seg_k_ref, kg0_ref, kc_ref, vc_ref):
    nc = kc_ref.shape[1]
    src = (k_ref, v_ref)
    read = lambda s: jnp.concatenate(
        [src[s][0, pl.ds(r, nc, stride=CMP_STRIDE), :] for r in range(CMP_STRIDE)], axis=1)
    kc, vc = _compress_core(read, nc, wr_ref, pe_ref, w2_ref, seg_k_ref, kg0_ref)
    kc_ref[0] = kc.astype(BF16)
    vc_ref[0] = vc.T.astype(BF16)


def _cmp_prompt(raw, wr, pe_aug, w2bd, seg_k, kg0):
    bx, t, _ = raw.shape
    nc = t // CMP_STRIDE
    full = lambda a: pl.BlockSpec(a.shape, lambda b: (0,) * a.ndim)
    return pl.pallas_call(
        _cmp_prompt_kernel,
        grid=(bx,),
        in_specs=[pl.BlockSpec((1, t, KV_W), lambda b: (b, 0, 0)),
                  pl.BlockSpec((1, t, KV_W), lambda b: (b, 0, 1)),
                  full(wr), full(pe_aug), full(w2bd), full(seg_k), full(kg0)],
        out_specs=[pl.BlockSpec((1, nc, KV_W), lambda b: (b, 0, 0)), pl.BlockSpec((1, KV_W, nc), lambda b: (b, 0, 0))],
        out_shape=[jax.ShapeDtypeStruct((bx, nc, KV_W), BF16), jax.ShapeDtypeStruct((bx, KV_W, nc), BF16)],
        compiler_params=_params("parallel"),
        name="cmp_prompt",
    )(raw, raw, wr, pe_aug, w2bd, seg_k, kg0)


def _cmp_sample_kernel(pt_ref, *refs, pps):
    pages = refs[:pps]
    perm_ref, wr_ref, pe_ref, w2_ref, seg_k_ref, kg0_ref, kc_ref, vc_ref, xs_ref = refs[pps:]
    step = pl.program_id(1)
    cpp = pages[0].shape[3] // CMP_STRIDE
    reorder = (lambda tile: _dot_nt(perm_ref[0], tile), lambda tile: _dot(tile, perm_ref[1]).T)
    for k in range(0, pps, 2):
        row0 = pl.multiple_of((step * pps + k) * cpp, 2 * cpp)
        for s in range(2):
            zt = [reorder[s](pages[k + j][0, 0, s * KV_W:(s + 1) * KV_W, :].astype(BF16))
                  for j in range(2)]
            for r in range(CMP_STRIDE):
                pair = jnp.concatenate([zt[0][r * cpp:(r + 1) * cpp], zt[1][r * cpp:(r + 1) * cpp]], axis=0)
                xs_ref[s, pl.ds(row0, 2 * cpp), r * KV_W:(r + 1) * KV_W] = pair.astype(BF16)

    @pl.when(step == pl.num_programs(1) - 1)
    def _():
        nc = kc_ref.shape[1]
        read = lambda s: xs_ref[s]
        kc, vc = _compress_core(read, nc, wr_ref, pe_ref, w2_ref, seg_k_ref, kg0_ref)
        kc_ref[0] = kc.astype(BF16)
        vc_ref[0] = vc.astype(BF16)


def _cmp_sample(cache_t, layer, page_table, perm, wr, pe_aug, w2bd, seg_k, kg0, pps):
    bx, n_pages = page_table.shape
    page = cache_t.shape[3]
    past = n_pages * page
    nc = past // CMP_STRIDE
    full = lambda a: pl.BlockSpec(a.shape, lambda b, s, pt: (0,) * a.ndim)
    page_specs = [pl.BlockSpec((1, 1, 2 * KV_W, page),
                               lambda b, s, pt, k=k: (layer, pt[b, s * pps + k], 0, 0))
                  for k in range(pps)]
    return pl.pallas_call(
        functools.partial(_cmp_sample_kernel, pps=pps),
        grid_spec=pltpu.PrefetchScalarGridSpec(
            num_scalar_prefetch=1,
            grid=(bx, n_pages // pps),
            in_specs=page_specs + [full(perm), full(wr), full(pe_aug), full(w2bd), full(seg_k), full(kg0)],
            out_specs=[pl.BlockSpec((1, nc, KV_W), lambda b, s, pt: (b, 0, 0))] * 2,
            scratch_shapes=[pltpu.VMEM((2, nc, CMP_STRIDE * KV_W), BF16)]),
        out_shape=[jax.ShapeDtypeStruct((bx, nc, KV_W), BF16)] * 2,
        compiler_params=_params("parallel", "arbitrary"),
        name="cmp_sample",
    )(page_table, *([cache_t] * pps), perm, wr, pe_aug, w2bd, seg_k, kg0)


def _q_rows(q, lane):
    out = []
    for g in range(N_KV_HEADS):
        keep = (lane // HEAD_DIM) == g
        rows = []
        for r in range(Q_PER_KV):
            h = g * Q_PER_KV + r
            blk = q[:, (h // 2) * LANE:(h // 2 + 1) * LANE]
            if h % 2 != g:
                blk = pltpu.roll(blk, HEAD_DIM, 1)
            rows.append(jnp.where(keep, blk, 0.0))
        out.append(jnp.concatenate(rows, axis=0).astype(BF16))
    return out


def _assemble_heads(vals, lane):
    cols = []
    for k in range(N_Q_HEADS // 2):
        a, b = vals[2 * k], vals[2 * k + 1]
        if (2 * k) // Q_PER_KV == 1:
            a = pltpu.roll(a, HEAD_DIM, 1)
        else:
            b = pltpu.roll(b, HEAD_DIM, 1)
        cols.append(jnp.where(lane < HEAD_DIM, a, b))
    return jnp.concatenate(cols, axis=-1)


def _gate_combine(gates, h, o_cmp, o_slc, o_win):
    c = h * N_BRANCH
    return gates[:, c:c + 1] * o_cmp + gates[:, c + 1:c + 2] * o_slc + gates[:, c + 2:c + 3] * o_win


def _softmax_parts(s3):
    m = jnp.max(s3, axis=-1, keepdims=True)
    e = jnp.exp2(s3 - m)
    return e, jnp.sum(e, axis=-1, keepdims=True)


def _online_update(m, l, acc, s3, pv):
    m_new = jnp.maximum(m, jnp.max(s3, axis=-1, keepdims=True))
    alpha = jnp.exp2(m - m_new)
    e = jnp.exp2(s3 - m_new)
    return m_new, alpha * l + jnp.sum(e, axis=-1, keepdims=True), alpha * acc + pv(e)


def _select_rows(imp, cur, n_sel):
    jidx = _iota(imp.shape, 1)
    valid = jidx <= cur
    forced = valid & ((jidx == 0) | (jidx == cur) | (jidx == cur - 1))
    sc = jnp.where(forced, FORCE, jnp.where(valid, imp, NEG))
    rank = jnp.zeros(imp.shape, jnp.int32)
    for i in range(n_sel):
        col = sc[:, i:i + 1]
        beats = (col > sc) | ((col == sc) & (jidx > i))
        rank = rank + jnp.where(beats, 1, 0)
    return (rank < min(N_SLC, n_sel)) & valid


def _select_cols(imp_t, cur, n_sel):
    jidx = _iota(imp_t.shape, 0)
    valid = jidx <= cur
    forced = valid & ((jidx == 0) | (jidx == cur) | (jidx == cur - 1))
    sc = jnp.where(forced, FORCE, jnp.where(valid, imp_t, NEG))
    rank = jnp.zeros(imp_t.shape, jnp.int32)
    for i in range(n_sel):
        row = sc[i:i + 1, :]
        beats = (row > sc) | ((row == sc) & (jidx > i))
        rank = rank + jnp.where(beats, 1, 0)
    return jnp.where((rank < min(N_SLC, n_sel)) & valid, 1.0, 0.0)


def _attn_prompt_kernel(q_ref, kc_ref, vct_ref, ksa_ref, vst_ref, kw_ref, vwt_ref, gates_ref, ovt_ref, o_ref):
    tq = q_ref.shape[1]
    nc = kc_ref.shape[1]
    chunk = vst_ref.shape[3]
    n_sel = ovt_ref.shape[0]
    cols = Q_PER_KV * tq
    groups = range(N_KV_HEADS)
    t0 = pl.program_id(1) * tq
    qpos = t0 + _iota((1, tq), 1)
    per_head = lambda a: jnp.concatenate([a] * Q_PER_KV, axis=1)
    q_t = q_ref[0].T
    zero = jnp.zeros((HEAD_DIM, tq), F32)
    qt = []
    for g in groups:
        blocks = []
        for r in range(Q_PER_KV):
            h = g * Q_PER_KV + r
            head = q_t[h * HEAD_DIM:(h + 1) * HEAD_DIM]
            blocks.append(jnp.concatenate([head, zero] if g == 0 else [zero, head], axis=0))
        qt.append(jnp.concatenate(blocks, axis=1).astype(BF16))
    row_lo = _iota((LANE, cols), 0) < HEAD_DIM
    row_half = _iota((LANE, 1), 0) // HEAD_DIM
    with_ones = lambda vt, g: jnp.where(row_half == g, vt, 1.0)

    def pair(a, normalised):
        o = jnp.where(row_lo, a[0], a[1])
        if normalised:
            return o
        return o / jnp.concatenate([a[0][HEAD_DIM:], a[1][:HEAD_DIM]], axis=0)

    cbias = per_head(jnp.where(_iota((nc, tq), 0) * CMP_STRIDE + CMP_BLOCK - 1 <= qpos, 0.0, NEG))
    cvalid = per_head(jnp.where(qpos >= CMP_BLOCK - 1, 1.0, 0.0))
    cur = qpos // SLC_BLOCK
    o_cmp, rhs = [], []
    for g in groups:
        s = _dot(kc_ref[0], qt[g]) + cbias
        e = jnp.exp2(s - jnp.max(s, axis=0, keepdims=True))
        p = e * (cvalid / jnp.sum(e, axis=0, keepdims=True))
        o_cmp.append(_dot(vct_ref[0], p.astype(BF16)))
        psum = p[:, 0:tq]
        for r in range(1, Q_PER_KV):
            psum = psum + p[:, r * tq:(r + 1) * tq]
        hi, lo = _split_hilo(psum)
        sel = _select_cols(_dot(ovt_ref[...], hi) + _dot(ovt_ref[...], lo), cur, n_sel)
        sel = jnp.concatenate([sel, jnp.zeros((LANE - n_sel, tq), F32)], axis=0)
        selneg = per_head(jnp.where(sel > 0.5, 0.0, NEG)).astype(BF16)
        rhs.append(jnp.concatenate([qt[g], selneg], axis=0))

    def slc_step(c, carry, causal):
        ka = ksa_ref[0, pl.ds(pl.multiple_of(c * chunk, chunk), chunk), :]
        vt = vst_ref[0, c]
        out = []
        for g in groups:
            m, acc = carry[g]
            s = _dot(ka, rhs[g])
            if causal is not None:
                s = jnp.where(causal, s, NEG)
            m_new = jnp.maximum(m, jnp.max(s, axis=0, keepdims=True))
            w = jnp.exp2(s - m_new).astype(BF16)
            out.append((m_new, jnp.exp2(m - m_new) * acc + _dot(with_ones(vt, g), w)))
        return tuple(out)

    init = tuple((jnp.full((1, cols), NEG, F32), jnp.zeros((LANE, cols), F32)) for g in groups)
    c_diag = t0 // chunk
    slc = lax.fori_loop(0, c_diag, lambda c, carry: slc_step(c, carry, None), init)
    slc = slc_step(c_diag, slc, per_head(c_diag * chunk + _iota((chunk, tq), 0) <= qpos))

    wtile0 = jnp.maximum(t0 - WINDOW, 0) // LANE
    wstart = pl.multiple_of(wtile0 * LANE, LANE)
    wn = WINDOW + tq
    wdiff = qpos - (wstart + _iota((wn, tq), 0))
    wbias = per_head(jnp.where((wdiff >= 0) & (wdiff < WINDOW), 0.0, NEG))
    kwin = kw_ref[0, pl.ds(wstart, wn), :]
    vwt = jnp.concatenate([vwt_ref[0, wtile0 + j] for j in range(wn // LANE)], axis=1)
    win = []
    for g in groups:
        s = _dot(kwin, qt[g]) + wbias
        w = jnp.exp2(s - jnp.max(s, axis=0, keepdims=True)).astype(BF16)
        win.append(_dot(with_ones(vwt, g), w))

    branches = (pair(o_cmp, True), pair([slc[g][1] for g in groups], False), pair(win, False))
    gt = gates_ref[0].T
    out = None
    for br in range(N_BRANCH):
        gate = jnp.concatenate(
            [jnp.concatenate([jnp.broadcast_to(gt[(g * Q_PER_KV + r) * N_BRANCH + br][None], (HEAD_DIM, tq))
                              for g in groups], axis=0) for r in range(Q_PER_KV)], axis=1)
        out = gate * branches[br] if out is None else out + gate * branches[br]
    heads = [out[g * HEAD_DIM:(g + 1) * HEAD_DIM, r * tq:(r + 1) * tq] for g in groups for r in range(Q_PER_KV)]
    o_ref[0] = jnp.concatenate(heads, axis=0).T.astype(BF16)


def _attn_prompt(q, kc, vct, ksa, vst, kw, vwt, gates, ovt):
    bx, t, _ = q.shape
    whole = lambda a: pl.BlockSpec((1,) + a.shape[1:], lambda b, i: (b,) + (0,) * (a.ndim - 1))
    const = lambda a: pl.BlockSpec(a.shape, lambda b, i: (0,) * a.ndim)
    return pl.pallas_call(
        _attn_prompt_kernel,
        grid=(bx, t // TQ),
        in_specs=[pl.BlockSpec((1, TQ, ATTN_W), lambda b, i: (b, i, 0)),
                  whole(kc), whole(vct), whole(ksa), whole(vst), whole(kw), whole(vwt),
                  pl.BlockSpec((1, TQ, GATE_PAD), lambda b, i: (b, i, 0)),
                  const(ovt)],
        out_specs=pl.BlockSpec((1, TQ, ATTN_W), lambda b, i: (b, i, 0)),
        out_shape=jax.ShapeDtypeStruct((bx, t, ATTN_W), BF16),
        compiler_params=_params("parallel", "parallel"),
        name="attn_prompt",
    )(q, kc, vct, ksa, vst, kw, vwt, gates, ovt)


def _attn_sample_a_kernel(q_ref, kc_ref, vc_ref, wst_ref, wnew_ref, ov_ref, e_ref, ocmp_ref, owin_ref, bias_ref,
                          wnext_ref, *, pos0, n_cmp, n_sel):
    t = q_ref.shape[1]
    nc = kc_ref.shape[1]
    n_state = wst_ref.shape[3]
    nr = N_Q_HEADS
    lane = _iota((t, LANE), 1)
    tcol = _iota((t, 1), 0)
    qpos = pos0 + tcol
    qbd = jnp.concatenate(_q_rows(q_ref[0], lane), axis=0)
    cidx = _iota((t, nc), 1)
    cbias = jnp.where((cidx < n_cmp) & ((cidx * CMP_STRIDE + CMP_BLOCK - 1) <= qpos), 0.0, NEG)
    cvalid = jnp.where(qpos >= CMP_BLOCK - 1, 1.0, 0.0)
    e, l = _softmax_parts(_dot_nt(qbd, kc_ref[0]).reshape(nr, t, nc) + cbias[None])
    p = e * (cvalid / l)
    ocmp_ref[0] = _dot(p.reshape(nr * t, nc).astype(BF16), vc_ref[0])
    p4 = p.reshape(N_KV_HEADS, Q_PER_KV, t, nc)
    psum = (p4[:, 0] + p4[:, 1] + p4[:, 2] + p4[:, 3]).reshape(N_KV_HEADS * t, nc)
    imp = _dot_hilo(psum, ov_ref[...])
    cur = jnp.concatenate([qpos // SLC_BLOCK] * N_KV_HEADS, axis=0)
    selneg = jnp.where(_select_rows(imp, cur, n_sel), 0.0, NEG).astype(BF16)
    bias_ref[0] = _dot(selneg, e_ref[...]).reshape(N_KV_HEADS, t, e_ref.shape[1])
    wst = wst_ref[0, 0]
    kwt = wst[0:KV_W].astype(BF16)
    vwt = wst[KV_W:2 * KV_W].astype(BF16)
    wnew = _pad_rows(wnew_ref[0], NEW_PAD).astype(BF16)
    sdiff = qpos - (pos0 - n_state + _iota((t, n_state), 1))
    sbias = jnp.where((sdiff >= 0) & (sdiff < WINDOW), 0.0, NEG)
    nbias = jnp.where(tcol - _iota((t, NEW_PAD), 1) >= 0, 0.0, NEG)
    s_a = _dot(qbd, kwt).reshape(nr, t, n_state) + sbias[None]
    s_b = _dot_nt(qbd, wnew[:, 0:KV_W]).reshape(nr, t, NEW_PAD) + nbias[None]
    m = jnp.maximum(jnp.max(s_a, axis=-1, keepdims=True), jnp.max(s_b, axis=-1, keepdims=True))
    e_a = jnp.exp2(s_a - m)
    e_b = jnp.exp2(s_b - m)
    l = jnp.sum(e_a, axis=-1, keepdims=True) + jnp.sum(e_b, axis=-1, keepdims=True)
    o = (_dot_nt(e_a.reshape(nr * t, n_state).astype(BF16), vwt)
         + _dot(e_b.reshape(nr * t, NEW_PAD).astype(BF16), wnew[:, KV_W:2 * KV_W]))
    owin_ref[0] = o / l.reshape(nr * t, 1)
    new_t = _pad_rows(wnew_ref[0], NEW_PAD).T
    tail = jnp.concatenate([jnp.zeros((2 * KV_W, n_state - NEW_PAD), F32), pltpu.roll(new_t, NEW_PAD - t, 1)],
                           axis=1)
    wnext_ref[0] = jnp.where(_iota((1, n_state), 1) < n_state - t, pltpu.roll(wst, n_state - t, 1), tail)


def _attn_sample_a(q, kc, vc, win_state_t, layer, win_new, ov, expand, pos0, n_cmp, n_sel):
    bx, t, _ = q.shape
    nc = kc.shape[1]
    n_state = win_state_t.shape[3]
    kp = expand.shape[1]
    rows = N_Q_HEADS * t
    per_b = lambda a: pl.BlockSpec((1,) + a.shape[1:], lambda b: (b,) + (0,) * (a.ndim - 1))
    const = lambda a: pl.BlockSpec(a.shape, lambda b: (0,) * a.ndim)
    return pl.pallas_call(
        functools.partial(_attn_sample_a_kernel, pos0=pos0, n_cmp=n_cmp, n_sel=n_sel),
        grid=(bx,),
        in_specs=[per_b(q), per_b(kc), per_b(vc),
                  pl.BlockSpec((1, 1, 2 * KV_W, n_state), lambda b: (layer, b, 0, 0)),
                  per_b(win_new), const(ov), const(expand)],
        out_specs=[pl.BlockSpec((1, rows, LANE), lambda b: (b, 0, 0)),
                   pl.BlockSpec((1, rows, LANE), lambda b: (b, 0, 0)),
                   pl.BlockSpec((1, N_KV_HEADS, t, kp), lambda b: (b, 0, 0, 0)),
                   pl.BlockSpec((1, 2 * KV_W, n_state), lambda b: (b, 0, 0))],
        out_shape=[jax.ShapeDtypeStruct((bx, rows, LANE), F32),
                   jax.ShapeDtypeStruct((bx, rows, LANE), F32),
                   jax.ShapeDtypeStruct((bx, N_KV_HEADS, t, kp), F32),
                   jax.ShapeDtypeStruct((bx, 2 * KV_W, n_state), F32)],
        compiler_params=_params("parallel"),
        name="attn_sample_a",
    )(q, kc, vc, win_state_t, win_new, ov, expand)


def _attn_sample_b_kernel(pt_ref, *refs, pps):
    pages = refs[:pps]
    (q_ref, kvn_ref, bias_ref, ocmp_ref, owin_ref, gates_ref, o_ref, s_ref, vt_ref) = refs[pps:]
    t = q_ref.shape[1]
    page = pages[0].shape[3]
    n = pps * page
    nr = N_Q_HEADS
    n_steps = s_ref.shape[0]
    step = pl.program_id(1)
    lane = _iota((t, LANE), 1)
    qbd = jnp.concatenate(_q_rows(q_ref[0], lane), axis=0)
    kt = jnp.concatenate([pages[k][0, 0, 0:KV_W, :].astype(BF16) for k in range(pps)], axis=1)
    s_ref[step] = _dot(qbd, kt)
    for k in range(pps):
        vt_ref[step, :, k * page:(k + 1) * page] = pages[k][0, 0, KV_W:2 * KV_W, :].astype(BF16)

    @pl.when(step == n_steps - 1)
    def _():
        biased = lambda s_flat, bias: (s_flat.reshape(N_KV_HEADS, Q_PER_KV, t, s_flat.shape[-1])
                                       + bias[:, None]).reshape(nr, t, s_flat.shape[-1])
        parts = [biased(s_ref[j], bias_ref[0, :, :, j * n:(j + 1) * n]) for j in range(n_steps)]
        kvn = _pad_rows(kvn_ref[0], NEW_PAD).astype(BF16)
        causal = _iota((t, NEW_PAD), 1) <= _iota((t, 1), 0)
        bias_new = jnp.where(causal[None], bias_ref[0, :, :, n_steps * n:n_steps * n + NEW_PAD], NEG)
        s_new = biased(_dot_nt(qbd, kvn[:, 0:KV_W]), bias_new)
        m = jnp.max(s_new, axis=-1, keepdims=True)
        for p in parts:
            m = jnp.maximum(m, jnp.max(p, axis=-1, keepdims=True))
        e = jnp.exp2(s_new - m)
        l = jnp.sum(e, axis=-1, keepdims=True)
        acc = _dot(e.reshape(nr * t, NEW_PAD).astype(BF16), kvn[:, KV_W:2 * KV_W])
        for j, p in enumerate(parts):
            e = jnp.exp2(p - m)
            l = l + jnp.sum(e, axis=-1, keepdims=True)
            acc = acc + _dot_nt(e.reshape(nr * t, n).astype(BF16), vt_ref[j])
        o_slc = acc.reshape(nr, t, LANE) / l
        gates = gates_ref[0]
        vals = [_gate_combine(gates, h, ocmp_ref[0, h * t:(h + 1) * t], o_slc[h], owin_ref[0, h * t:(h + 1) * t])
                for h in range(nr)]
        o_ref[0] = _assemble_heads(vals, lane)


def _attn_sample_b(cache_t, layer, page_table, q, kv_new, bias, o_cmp, o_win, gates, pps):
    bx, n_pages = page_table.shape
    page = cache_t.shape[3]
    t = q.shape[1]
    rows = N_Q_HEADS * t
    n = pps * page
    page_specs = [pl.BlockSpec((1, 1, 2 * KV_W, page),
                               lambda b, s, pt, k=k: (layer, pt[b, s * pps + k], 1, 0))
                  for k in range(pps)]
    return pl.pallas_call(
        functools.partial(_attn_sample_b_kernel, pps=pps),
        grid_spec=pltpu.PrefetchScalarGridSpec(
            num_scalar_prefetch=1,
            grid=(bx, n_pages // pps),
            in_specs=page_specs + [
                pl.BlockSpec((1, t, ATTN_W), lambda b, s, pt: (b, 0, 0)),
                pl.BlockSpec((1, t, 2 * KV_W), lambda b, s, pt: (b, 0, 1)),
                pl.BlockSpec((1,) + bias.shape[1:], lambda b, s, pt: (b, 0, 0, 0)),
                pl.BlockSpec((1, rows, LANE), lambda b, s, pt: (b, 0, 0)),
                pl.BlockSpec((1, rows, LANE), lambda b, s, pt: (b, 0, 0)),
                pl.BlockSpec((1, t, GATE_PAD), lambda b, s, pt: (b, 0, 0))],
            out_specs=pl.BlockSpec((1, t, ATTN_W), lambda b, s, pt: (b, 0, 0)),
            scratch_shapes=[pltpu.VMEM((n_pages // pps, rows, n), F32),
                            pltpu.VMEM((n_pages // pps, KV_W, n), BF16)]),
        out_shape=jax.ShapeDtypeStruct((bx, t, ATTN_W), F32),
        compiler_params=_params("parallel", "arbitrary"),
        name="attn_sample_b",
    )(page_table, *([cache_t] * pps), q, kv_new, bias, o_cmp, o_win, gates)


def _mix_out_kernel(x_ref, mod_ref, oattn_ref, bg_ref, u_ref, uh_ref, p_ref, ph_ref, cw_ref, cb_ref,
                    pw_ref, ps_ref, wo_ref, norm_ref, x1_ref, h2_ref, *, pos0, zero_first_halo):
    bb, tt, d = x_ref.shape
    rows = bb * tt
    i = pl.program_id(1)
    uh = uh_ref[...]
    ph = ph_ref[...]
    if zero_first_halo:
        keep = jnp.where(i > 0, 1.0, 0.0)
        uh = uh * keep
        ph = ph * keep
    ucat = jnp.concatenate([uh, u_ref[...]], axis=1)
    conv = cb_ref[...]
    for j in range(CONV_K):
        off = CONV_HALO - CONV_BUF + j
        conv = conv + ucat[:, off:off + tt] * cw_ref[j:j + 1]
    y_conv = bg_ref[...] * conv
    pcat = jnp.concatenate([ph, p_ref[...]], axis=1)
    a2 = pcat[:, 1:] + pcat[:, :-1]
    a4 = a2[:, 2:] + a2[:, :-2]
    a8 = a4[:, 4:] + a4[:, :-4]
    a16 = a8[:, 8:] + a8[:, :-8]
    sums = (a2[:, POOL_HALO - 1:POOL_HALO - 1 + tt], a4[:, POOL_HALO - 3:POOL_HALO - 3 + tt],
            a8[:, POOL_HALO - 7:POOL_HALO - 7 + tt], a16[:, POOL_HALO - 15:POOL_HALO - 15 + tt])
    pos = pos0 + i * tt + _iota((1, tt, 1), 1)
    grp = _iota((1, 1, POOL_W), 2) // (POOL_W // len(POOL_WINDOWS))
    mean = jnp.zeros((bb, tt, POOL_W), F32)
    for gi, w in enumerate(POOL_WINDOWS):
        cnt = jnp.minimum(w, pos + 1).astype(F32)
        mean = jnp.where(grp == gi, sums[gi] / cnt, mean)
    dlt = (mean - p_ref[...]).reshape(rows, POOL_W).astype(BF16)
    y_pool = _dot(dlt, pw_ref[...]) * ps_ref[...]
    mix = _dot(oattn_ref[...].reshape(rows, ATTN_W).astype(BF16), wo_ref[0, 0:ATTN_W])
    mix = mix + _dot(y_conv.reshape(rows, CONV_W).astype(BF16), wo_ref[0, ATTN_W:ATTN_W + CONV_W])
    mix = mix + _dot(y_pool.astype(BF16), wo_ref[0, ATTN_W + CONV_W:ATTN_W + CONV_W + POOL_W])
    gate1 = mod_ref[:, :, 2 * d:3 * d]
    x1 = x_ref[...] + gate1 * mix.reshape(bb, tt, d)
    x1_ref[...] = x1
    y = x1 * lax.rsqrt(jnp.mean(x1 * x1, axis=-1, keepdims=True) + EPS) * norm_ref[...]
    h2_ref[...] = (y * (1.0 + mod_ref[:, :, 4 * d:5 * d]) + mod_ref[:, :, 3 * d:4 * d]).astype(h2_ref.dtype)


def _mix_out(x, mod, oattn, bg, u, u_halo, pool, p_halo, conv_w, conv_b, pool_wbd, pool_scale, w_out, layer, norm2,
             bb, tt, pos0, halo_from_self):
    bx, t, d = x.shape
    wo_spec = pl.BlockSpec((1,) + w_out.shape[1:], lambda b, i: (layer, 0, 0), pipeline_mode=pl.Buffered(1))
    blk = lambda w: pl.BlockSpec((bb, tt, w), lambda b, i: (b, i, 0))
    full = lambda a: pl.BlockSpec(a.shape, lambda b, i: (0,) * a.ndim)
    if halo_from_self:
        uh_spec = pl.BlockSpec((bb, CONV_HALO, CONV_W),
                               lambda b, i: (b, jnp.maximum(i * (tt // CONV_HALO) - 1, 0), 0))
        ph_spec = pl.BlockSpec((bb, POOL_HALO, POOL_W),
                               lambda b, i: (b, jnp.maximum(i * (tt // POOL_HALO) - 1, 0), 0))
    else:
        uh_spec = pl.BlockSpec((bb, CONV_HALO, CONV_W), lambda b, i: (b, 0, 0))
        ph_spec = pl.BlockSpec((bb, POOL_HALO, POOL_W), lambda b, i: (b, 0, 0))
    return pl.pallas_call(
        functools.partial(_mix_out_kernel, pos0=pos0, zero_first_halo=halo_from_self),
        grid=(bx // bb, t // tt),
        in_specs=[blk(d), pl.BlockSpec((bb, 1, mod.shape[-1]), lambda b, i: (b, 0, 0)),
                  blk(ATTN_W), blk(CONV_W), blk(CONV_W), uh_spec, blk(POOL_W), ph_spec,
                  full(conv_w), full(conv_b), full(pool_wbd), full(pool_scale), wo_spec, full(norm2)],
        out_specs=[blk(d), blk(d)],
        out_shape=[jax.ShapeDtypeStruct((bx, t, d), F32), jax.ShapeDtypeStruct((bx, t, d), oattn.dtype)],
        compiler_params=_params("parallel", "parallel"),
        name="mix_out",
    )(x, mod, oattn, bg, u, u_halo, pool, p_halo, conv_w, conv_b, pool_wbd, pool_scale, w_out, norm2)


def _ffn_kernel(x1_ref, h2_ref, mod_ref, wa_ref, wb_ref, wd_ref, o_ref):
    bb, tt, d = x1_ref.shape
    h = h2_ref[...].reshape(bb * tt, d).astype(BF16)
    a = _dot(h, wa_ref[0])
    b = _dot(h, wb_ref[0])
    y = _dot((_silu(a) * b).astype(BF16), wd_ref[0])
    o_ref[...] = x1_ref[...] + mod_ref[:, :, 5 * d:6 * d] * y.reshape(bb, tt, d)


def _ffn(x1, h2, mod, w_up, w_down, layer, bb, tt):
    bx, t, d = x1.shape
    hid = w_down.shape[1]
    blk = pl.BlockSpec((bb, tt, d), lambda b, i: (b, i, 0))
    resident = lambda shape, imap: pl.BlockSpec(shape, imap, pipeline_mode=pl.Buffered(1))
    return pl.pallas_call(
        _ffn_kernel,
        grid=(bx // bb, t // tt),
        in_specs=[blk, blk, pl.BlockSpec((bb, 1, mod.shape[-1]), lambda b, i: (b, 0, 0)),
                  resident((1, d, hid), lambda b, i: (layer, 0, 0)),
                  resident((1, d, hid), lambda b, i: (layer, 0, 1)),
                  resident((1, hid, d), lambda b, i: (layer, 0, 0))],
        out_specs=blk,
        out_shape=jax.ShapeDtypeStruct((bx, t, d), F32),
        compiler_params=_params("parallel", "parallel"),
        name="ffn",
    )(x1, h2, mod, w_up, w_up, w_down)


def _overlap_matrix(n_cmp, n_sel, rows, lanes):
    m = np.zeros((rows, lanes), np.float32)
    i = np.arange(n_cmp)
    for part in range(CMP_BLOCK // CMP_STRIDE):
        j = np.minimum((i + part) * CMP_STRIDE // SLC_BLOCK, n_sel - 1)
        np.add.at(m, (i, j), 1.0)
    return m


def _expand_matrix(n_blocks, n_keys):
    return (np.arange(n_blocks)[:, None] == np.arange(n_keys)[None, :] // SLC_BLOCK).astype(np.float32)


def _chunk_order_perm(page):
    pos = np.arange(page)
    m = np.zeros((page, page), np.float32)
    m[(pos % CMP_STRIDE) * (page // CMP_STRIDE) + pos // CMP_STRIDE, pos] = 1.0
    return jnp.asarray(np.stack([m, m.T]), BF16)


def _seg_matrix(n):
    idx = np.arange(n) // HEAD_DIM
    return jnp.asarray((idx[:, None] == idx[None, :]).astype(np.float32) / HEAD_DIM, BF16)


def _prep_weights(w_in, w_out, q_norm, k_norm, cmp_pe, cmp_w1, cmp_w2, pool_w, w_up, w_down):
    depth, d, _ = w_in.shape
    g0 = ATTN_W + 6 * KV_W
    w_in_r = jnp.concatenate([w_in[:, :, :g0], w_in[:, :, g0 + N_Q_HEADS * N_BRANCH:],
                              w_in[:, :, g0:g0 + N_Q_HEADS * N_BRANCH],
                              jnp.zeros((depth, d, GATE_PAD - N_Q_HEADS * N_BRANCH), w_in.dtype)], axis=-1)
    eye_g = jnp.eye(N_KV_HEADS, dtype=BF16)
    w1r = cmp_w1.astype(BF16).reshape(depth, 2, 2, CMP_STRIDE, HEAD_DIM, HEAD_DIM)
    zero = jnp.zeros_like(w1r[:, :, 0])
    wr = jnp.stack([jnp.concatenate([w1r[:, :, 0], zero, w1r[:, :, 1], zero], axis=-1),
                    jnp.concatenate([zero, w1r[:, :, 0], zero, w1r[:, :, 1]], axis=-1)], axis=3)
    wr = wr.reshape(depth, 2, CMP_STRIDE * KV_W, 2 * KV_W)
    w2bd = (cmp_w2.astype(BF16)[:, :, None, :, None, :]
            * eye_g[None, None, :, None, :, None]).reshape(depth, 2, KV_W, KV_W)
    pe = cmp_pe.reshape(depth, 2, 2, CMP_STRIDE, 1, HEAD_DIM)
    pe = jnp.broadcast_to(pe, (depth, 2, 2, CMP_STRIDE, N_KV_HEADS, HEAD_DIM))
    pe = pe.reshape(depth, 2, 2, CMP_STRIDE * KV_W)
    pe_aug = jnp.concatenate([pe, jnp.zeros((depth, 2, 6, CMP_STRIDE * KV_W), F32)], axis=2)
    eye_p = jnp.eye(len(POOL_WINDOWS), dtype=F32)
    pool_bd = (pool_w[:, :, :, None, :] * eye_p[None, :, None, :, None]).reshape(depth, POOL_W, POOL_W)
    return dict(
        w_in=w_in_r.astype(BF16), w_out=w_out.astype(BF16), w_up=w_up.astype(BF16), w_down=w_down.astype(BF16),
        wr=wr.astype(BF16), pe_aug=pe_aug, w2bd=w2bd.astype(BF16), pool_bd=pool_bd.astype(BF16),
        q_gain=jnp.tile(q_norm, (1, N_Q_HEADS))[:, None, :],
        k_gain=jnp.tile(k_norm, (1, 1, N_KV_HEADS)),
    )


def _feature_major_view(a):
    lead = a.shape[:-2]
    f, p = a.shape[-2:]
    a = a.reshape(lead + (f // KV_W, N_KV_HEADS, HEAD_DIM, p))
    n = len(lead)
    return jnp.transpose(a, tuple(range(n)) + (n + 3, n, n + 1, n + 2))


def _layer_prompt(x, mod, l, W, P, C):
    bx, t, d = x.shape
    bb, tt = 1, ROW_TILE
    kg = P["k_gain"][l]
    q, bg, u, pool, gates, kv_t, win_t, raw, ksa, vst, kw, vwt = _proj_in(
        x, mod, W["norm_mix"][l][None, None], P["w_in"], l, P["q_gain"][l], kg[1:3],
        C["seg_q"], C["seg_k"], bb, tt, True)
    kc, vct = _cmp_prompt(raw, P["wr"][l], P["pe_aug"][l], P["w2bd"][l], C["seg_k"], kg[0:1])
    oattn = _attn_prompt(q, kc, vct, ksa, vst, kw, vwt, gates, C["ovt_prompt"])
    x1, h2 = _mix_out(x, mod, oattn, bg, u, u, pool, pool, W["conv_w"][l], W["conv_bias"][l][None],
                      P["pool_bd"][l], W["pool_scale"][l][None], P["w_out"], l, W["norm_ffn"][l][None, None],
                      bb, tt, 0, True)
    x2 = _ffn(x1, h2, mod, P["w_up"], P["w_down"], l, bb, tt)
    states = (kv_t, win_t, u[:, t - CONV_BUF:], pool[:, t - POOL_BUF:])
    return x2, states


def _layer_sample(x, mod, l, W, P, C, cache_t, page_table, win_state_t, conv_state, pool_state):
    bx, t, d = x.shape
    n_pages = page_table.shape[1]
    pos0 = n_pages * cache_t.shape[3]
    pps = C["pps"]
    kg = P["k_gain"][l]
    q, bg, u, pool, gates, kv, win = _proj_in(
        x, mod, W["norm_mix"][l][None, None], P["w_in"], l, P["q_gain"][l], kg[1:3],
        C["seg_q"], C["seg_k"], bx, t, False)
    kc, vc = _cmp_sample(cache_t, l, page_table, C["perm"], P["wr"][l], P["pe_aug"][l], P["w2bd"][l],
                         C["seg_k"], kg[0:1], pps)
    o_cmp, o_win, bias, win_t = _attn_sample_a(q, kc, vc, win_state_t, l, win, C["ov_sample"], C["expand_sample"],
                                               pos0, C["n_cmp_s"], C["n_sel_s"])
    oattn = _attn_sample_b(cache_t, l, page_table, q, kv, bias, o_cmp, o_win, gates, pps)
    u_halo = jnp.concatenate([jnp.zeros((bx, CONV_HALO - CONV_BUF, CONV_W), F32), conv_state], axis=1)
    p_halo = jnp.concatenate([jnp.zeros((bx, POOL_HALO - POOL_BUF, POOL_W), F32), pool_state], axis=1)
    x1, h2 = _mix_out(x, mod, oattn, bg, u, u_halo, pool, p_halo, W["conv_w"][l], W["conv_bias"][l][None],
                      P["pool_bd"][l], W["pool_scale"][l][None], P["w_out"], l, W["norm_ffn"][l][None, None],
                      bx, t, pos0, False)
    x2 = _ffn(x1, h2, mod, P["w_up"], P["w_down"], l, bx, t)
    states =(kv.reshape(bx, t, 4, N_KV_HEADS, HEAD_DIM), win_t,
              jnp.concatenate([conv_state, u], axis=1)[:, t:],
              jnp.concatenate([pool_state, pool], axis=1)[:, t:])
    return x2, states


def kernel(x_prompt, x_sample, cache_nsa_kv, state_win_kv, state_conv, state_pool, page_table, c_prompt, c_sample, norm_mix, norm_ffn, w_ada, b_ada, w_in, w_out, q_norm, k_norm, cmp_pe, cmp_w1, cmp_w2, conv_w, conv_bias, pool_w, pool_scale, w_up, w_down):
    depth = w_in.shape[0]
    bp, tp, d = x_prompt.shape
    bs, ts, _ = x_sample.shape
    n_pages = page_table.shape[1]
    n_pool, page = cache_nsa_kv.shape[1:3]
    past = n_pages * page
    n_state = state_win_kv.shape[2]
    hid = w_down.shape[1]
    assert tp % ROW_TILE == 0 and tp >= WINDOW + TQ and past % SLC_BLOCK == 0 and ts <= CONV_HALO

    W = dict(norm_mix=norm_mix, norm_ffn=norm_ffn, conv_w=conv_w, conv_bias=conv_bias, pool_scale=pool_scale)
    P = _prep_weights(w_in, w_out, q_norm, k_norm, cmp_pe, cmp_w1, cmp_w2, pool_w, w_up, w_down)
    n_cmp_p = (tp - CMP_BLOCK) // CMP_STRIDE + 1
    n_sel_p = tp // SLC_BLOCK
    n_cmp_s = (past + ts - CMP_BLOCK) // CMP_STRIDE + 1
    n_sel_s = -(-(past + ts) // SLC_BLOCK)
    sel_lanes = -(-n_sel_s // LANE) * LANE
    pps = min(PAGES_PER_STEP, n_pages)
    C = dict(
        seg_q=_seg_matrix(ATTN_W), seg_k=_seg_matrix(KV_W), perm=_chunk_order_perm(page),
        ovt_prompt=jnp.asarray(_overlap_matrix(n_cmp_p, n_sel_p, tp // CMP_STRIDE, n_sel_p).T, BF16),
        ov_sample=jnp.asarray(_overlap_matrix(n_cmp_s, n_sel_s, past // CMP_STRIDE, sel_lanes), BF16),
        expand_sample=jnp.asarray(_expand_matrix(sel_lanes, past + NEW_PAD), BF16),
        n_cmp_s=n_cmp_s, n_sel_s=n_sel_s, pps=pps,
    )
    cache_t = jnp.transpose(cache_nsa_kv, (0, 1, 3, 4, 5, 2)).reshape(depth, n_pool, 4 * KV_W, page)
    win_state_t = jnp.transpose(state_win_kv, (0, 1, 3, 4, 5, 2)).reshape(depth, bs, 2 * KV_W, n_state)
    mod = _ada(jnp.concatenate([c_prompt, c_sample], axis=0), w_ada, b_ada)
    mod_p = mod[:, :bp, None, :]
    mod_s = mod[:, bp:, None, :]

    yp, ys = x_prompt, x_sample
    outs_p, outs_s = [], []
    for l in range(depth):
        yp, st = _layer_prompt(yp, mod_p[l], l, W, P, C)
        outs_p.append(st)
        ys, st = _layer_sample(ys, mod_s[l], l, W, P, C, cache_t, page_table, win_state_t,
                               state_conv[l], state_pool[l])
        outs_s.append(st)
    stack = lambda outs, k: jnp.stack([o[k] for o in outs])
    return (yp, ys, _feature_major_view(stack(outs_p, 0)), stack(outs_s, 0),
            _feature_major_view(stack(outs_p, 1)), _feature_major_view(stack(outs_s, 1)),
            stack(outs_p, 2), stack(outs_s, 2), stack(outs_p, 3), stack(outs_s, 3))
```

```python
import functools

import numpy as np
import jax
import jax.numpy as jnp
from jax import lax
from jax.experimental import pallas as pl
from jax.experimental.pallas import tpu as pltpu

F32 = jnp.float32
BF16 = jnp.bfloat16

HEAD_DIM = 64
N_Q_HEADS = 8
N_KV_HEADS = 2
Q_PER_KV = N_Q_HEADS // N_KV_HEADS
N_BRANCH = 3
ATTN_W = N_Q_HEADS * HEAD_DIM
KV_W = N_KV_HEADS * HEAD_DIM
CONV_W = 256
POOL_W = 256
CONV_K = 3
CONV_BUF = CONV_K - 1
POOL_WINDOWS = (2, 4, 8, 16)
POOL_BUF = max(POOL_WINDOWS) - 1
CMP_BLOCK = 32
CMP_STRIDE = 16
SLC_BLOCK = 64
N_SLC = 16
WINDOW = 512
EPS = 1e-6
NEG = -1e30
FORCE = 1e4
ATTN_SCALE = HEAD_DIM ** -0.5
LOG2E = 1.4426950408889634
LANE = 128
GATE_PAD = LANE
IN_W_PAD = ATTN_W + 4 * KV_W + 2 * KV_W + 3 * CONV_W + POOL_W + GATE_PAD
CONV_HALO = 8
POOL_HALO = 16
NEW_PAD = LANE
VMEM_LIMIT = 56 * 1024 * 1024
ROW_TILE = 512
TQ = 512
SLC_CHUNK = 512
WIN_SUB = 256
PAGES_PER_STEP = 64
ADA_TN = 1536

_NT = (((1,), (1,)), ((), ()))


def _params(*sem):
    return pltpu.CompilerParams(dimension_semantics=sem, vmem_limit_bytes=VMEM_LIMIT)


def _dot(a, b):
    return jnp.dot(a, b, preferred_element_type=F32)


def _dot_nt(a, b):
    return lax.dot_general(a, b, _NT, preferred_element_type=F32)


def _split_hilo(x):
    hi = x.astype(BF16)
    return hi, (x - hi.astype(F32)).astype(BF16)


def _dot_hilo(x, w_bf):
    hi, lo = _split_hilo(x)
    return _dot(hi, w_bf) + _dot(lo, w_bf)


def _seg_rms(x, seg_bf, gain):
    ms = _dot_hilo(x * x, seg_bf)
    return x * lax.rsqrt(ms + EPS) * gain


def _silu(x):
    return x * jax.nn.sigmoid(x)


def _pad_rows(x, n):
    return jnp.concatenate([x, jnp.zeros((n - x.shape[0], x.shape[1]), x.dtype)], axis=0)


def _iota(shape, axis):
    return lax.broadcasted_iota(jnp.int32, shape, axis)


def _ada_kernel(c_ref, w_ref, b_ref, o_ref):
    c = c_ref[...]
    o_ref[0] = _dot(_silu(c).astype(BF16), w_ref[0].astype(BF16)) + b_ref[0]


def _ada(c_all, w_ada, b_ada):
    depth, d, n = w_ada.shape
    r = c_all.shape[0]
    return pl.pallas_call(
        _ada_kernel,
        grid=(depth, n // ADA_TN),
        in_specs=[pl.BlockSpec((r, d), lambda l, j: (0, 0)),
                  pl.BlockSpec((1, d, ADA_TN), lambda l, j: (l, 0, j)),
                  pl.BlockSpec((1, 1, ADA_TN), lambda l, j: (l, 0, j))],
        out_specs=pl.BlockSpec((1, r, ADA_TN), lambda l, j: (l, 0, j)),
        out_shape=jax.ShapeDtypeStruct((depth, r, n), F32),
        compiler_params=_params("parallel", "parallel"),
        name="ada",
    )(c_all, w_ada, b_ada.reshape(depth, 1, n))


def _proj_in_kernel(x_ref, mod_ref, norm_ref, w_ref, qg_ref, kg_ref, seg_q_ref, seg_k_ref,
                    q_ref, bg_ref, u_ref, pool_ref, gates_ref, *kv_refs, feature_major):
    bb, tt, d = x_ref.shape
    rows = bb * tt
    x = x_ref[...]
    y = x * lax.rsqrt(jnp.mean(x * x, axis=-1, keepdims=True) + EPS) * norm_ref[...]
    shift1 = mod_ref[:, :, 0:d]
    scale1 = mod_ref[:, :, d:2 * d]
    h = (y * (1.0 + scale1) + shift1).reshape(rows, d).astype(BF16)
    z = _dot(h, w_ref[0])

    qn = _seg_rms(z[:, 0:ATTN_W], seg_q_ref[...], qg_ref[...]) * (ATTN_SCALE * LOG2E)
    q_ref[...] = qn.reshape(bb, tt, ATTN_W)
    o = ATTN_W
    raw_cmp = z[:, o:o + 2 * KV_W]
    ks = _seg_rms(z[:, o + 2 * KV_W:o + 3 * KV_W], seg_k_ref[...], kg_ref[0:1])
    vs = z[:, o + 3 * KV_W:o + 4 * KV_W]
    kv = jnp.concatenate([raw_cmp, ks, vs], axis=-1)
    o += 4 * KV_W
    kw = _seg_rms(z[:, o:o + KV_W], seg_k_ref[...], kg_ref[1:2])
    vw = z[:, o + KV_W:o + 2 * KV_W]
    win = jnp.concatenate([kw, vw], axis=-1)
    o += 2 * KV_W
    if feature_major:
        kvt_ref, wint_ref, raw_ref, ksa_ref, vst_ref, kw_ref, vwt_ref = kv_refs
        kvt = kv.T
        wint = win.T
        kvt_ref[0] = kvt
        wint_ref[0] = wint
        raw_ref[0] = raw_cmp
        key_block = (pl.program_id(1) * rows + _iota((rows, LANE), 0)) // SLC_BLOCK
        onehot = jnp.where(key_block == _iota((rows, LANE), 1), 1.0, 0.0)
        ksa_ref[0] = jnp.concatenate([ks, onehot], axis=-1).astype(BF16)
        vst = kvt[3 * KV_W:4 * KV_W].astype(BF16)
        for j in range(rows // SLC_CHUNK):
            vst_ref[0, j] = vst[:, j * SLC_CHUNK:(j + 1) * SLC_CHUNK]
        kw_ref[0] = kw.astype(BF16)
        vwt = wint[KV_W:2 * KV_W].astype(BF16)
        for j in range(rows // LANE):
            vwt_ref[0, j] = vwt[:, j * LANE:(j + 1) * LANE]
    else:
        kv_ref, win_ref = kv_refs
        kv_ref[...] = kv.reshape(bb, tt, 4 * KV_W)
        win_ref[...] = win.reshape(bb, tt, 2 * KV_W)
    bg_ref[...] = z[:, o:o + CONV_W].reshape(bb, tt, CONV_W)
    u_ref[...] = (z[:, o + CONV_W:o + 2 * CONV_W] * z[:, o + 2 * CONV_W:o + 3 * CONV_W]).reshape(bb, tt, CONV_W)
    o += 3 * CONV_W
    pool_ref[...] = z[:, o:o + POOL_W].reshape(bb, tt, POOL_W)
    o += POOL_W
    gates_ref[...] = jax.nn.sigmoid(z[:, o:o + GATE_PAD]).reshape(bb, tt, GATE_PAD)


def _proj_in(x, mod, norm1, w_in, layer, q_gain, k_gain2, seg_q, seg_k, bb, tt, feature_major):
    bx, t, d = x.shape
    w_spec = pl.BlockSpec((1,) + w_in.shape[1:], lambda b, i: (layer, 0, 0), pipeline_mode=pl.Buffered(1))
    grid = (bx // bb, t // tt)
    blk = lambda w: pl.BlockSpec((bb, tt, w), lambda b, i: (b, i, 0))
    full = lambda a: pl.BlockSpec(a.shape, lambda b, i: (0,) * a.ndim)
    outs = [((bx, t, ATTN_W), F32), ((bx, t, CONV_W), F32), ((bx, t, CONV_W), F32), ((bx, t, POOL_W), F32),
            ((bx, t, GATE_PAD), F32)]
    out_specs = [blk(s[-1]) for s, _ in outs]
    if feature_major:
        assert bb == 1 and tt % SLC_CHUNK == 0 and SLC_CHUNK % LANE == 0
        assert tt == min(WINDOW, t)
        outs += [((bx, 4 * KV_W, t), F32), ((bx, 2 * KV_W, tt), F32), ((bx, t, 2 * KV_W), F32),
                 ((bx, t, 2 * KV_W), BF16), ((bx, t // SLC_CHUNK, KV_W, SLC_CHUNK), BF16),
                 ((bx, t, KV_W), BF16), ((bx, t // LANE, KV_W, LANE), BF16)]
        out_specs += [pl.BlockSpec((1, 4 * KV_W, tt), lambda b, i: (b, 0, i)),
                      pl.BlockSpec((1, 2 * KV_W, tt), lambda b, i: (b, 0, 0)),
                      blk(2 * KV_W),
                      blk(2 * KV_W),
                      pl.BlockSpec((1, tt // SLC_CHUNK, KV_W, SLC_CHUNK), lambda b, i: (b, i, 0, 0)),
                      blk(KV_W),
                      pl.BlockSpec((1, tt // LANE, KV_W, LANE), lambda b, i: (b, i, 0, 0))]
    else:
        outs += [((bx, t, 4 * KV_W), F32), ((bx, t, 2 * KV_W), F32)]
        out_specs += [blk(4 * KV_W), blk(2 * KV_W)]
    return pl.pallas_call(
        functools.partial(_proj_in_kernel, feature_major=feature_major),
        grid=grid,
        in_specs=[blk(d), pl.BlockSpec((bb, 1, mod.shape[-1]), lambda b, i: (b, 0, 0)),
                  full(norm1), w_spec, full(q_gain), full(k_gain2), full(seg_q), full(seg_k)],
        out_specs=out_specs,
        out_shape=[jax.ShapeDtypeStruct(s, dt) for s, dt in outs],
        compiler_params=_params("parallel", "arbitrary" if feature_major else "parallel"),
        name="proj_in",
    )(x, mod, norm1, w_in, q_gain, k_gain2, seg_q, seg_k)


def _compress_core(chunks, nc, wr_ref, pe_ref, w2_ref, seg_k_ref, kg0_ref):
    accs = [_dot(chunks(s).astype(BF16), wr_ref[s]) for s in range(2)]
    outs = []
    for s in range(2):
        acc = accs[s]
        acc_pe = _dot(pe_ref[s].astype(BF16), wr_ref[s])
        bias = acc_pe[0:1, 0:KV_W] + acc_pe[1:2, KV_W:2 * KV_W]
        nxt = pltpu.roll(acc[:, KV_W:2 * KV_W], nc - 1, 0)
        pre = acc[:, 0:KV_W] + nxt + bias
        outs.append(_dot(_silu(pre).astype(BF16), w2_ref[s]))
    return _seg_rms(outs[0], seg_k_ref[...], kg0_ref[...]), outs[1]


def _cmp_prompt_kernel(k_ref, v_ref, wr_ref, pe_ref, w2_ref, seg_k_ref, kg0_ref, kc_ref, vc_ref):
    nc = kc_ref.shape[1]
    src = (k_ref, v_ref)
    read = lambda s: jnp.concatenate(
        [src[s][0, pl.ds(r, nc, stride=CMP_STRIDE), :] for r in range(CMP_STRIDE)], axis=1)
    kc, vc = _compress_core(read, nc, wr_ref, pe_ref, w2_ref, seg_k_ref, kg0_ref)
    kc_ref[0] = kc.astype(BF16)
    vc_ref[0] = vc.T.astype(BF16)


def _cmp_prompt(raw, wr, pe_aug, w2bd, seg_k, kg0):
    bx, t, _ = raw.shape
    nc = t // CMP_STRIDE
    full = lambda a: pl.BlockSpec(a.shape, lambda b: (0,) * a.ndim)
    return pl.pallas_call(
        _cmp_prompt_kernel,
        grid=(bx,),
        in_specs=[pl.BlockSpec((1, t, KV_W), lambda b: (b, 0, 0)),
                  pl.BlockSpec((1, t, KV_W), lambda b: (b, 0, 1)),
                  full(wr), full(pe_aug), full(w2bd), full(seg_k), full(kg0)],
        out_specs=[pl.BlockSpec((1, nc, KV_W), lambda b: (b, 0, 0)), pl.BlockSpec((1, KV_W, nc), lambda b: (b, 0, 0))],
        out_shape=[jax.ShapeDtypeStruct((bx, nc, KV_W), BF16), jax.ShapeDtypeStruct((bx, KV_W, nc), BF16)],
        compiler_params=_params("parallel"),
        name="cmp_prompt",
    )(raw, raw, wr, pe_aug, w2bd, seg_k, kg0)


def _cmp_sample_kernel(pt_ref, *refs, pps):
    pages = refs[:pps]
    perm_ref, wr_ref, pe_ref, w2_ref, seg_k_ref, kg0_ref, kc_ref, vc_ref, xs_ref = refs[pps:]
    step = pl.program_id(1)
    cpp = pages[0].shape[3] // CMP_STRIDE
    reorder = (lambda tile: _dot_nt(perm_ref[0], tile), lambda tile: _dot(tile, perm_ref[1]).T)
    for k in range(0, pps, 2):
        row0 = pl.multiple_of((step * pps + k) * cpp, 2 * cpp)
        for s in range(2):
            zt = [reorder[s](pages[k + j][0, 0, s * KV_W:(s + 1) * KV_W, :].astype(BF16))
                  for j in range(2)]
            for r in range(CMP_STRIDE):
                pair = jnp.concatenate([zt[0][r * cpp:(r + 1) * cpp], zt[1][r * cpp:(r + 1) * cpp]], axis=0)
                xs_ref[s, pl.ds(row0, 2 * cpp), r * KV_W:(r + 1) * KV_W] = pair.astype(BF16)

    @pl.when(step == pl.num_programs(1) - 1)
    def _():
        nc = kc_ref.shape[1]
        read = lambda s: xs_ref[s]
        kc, vc = _compress_core(read, nc, wr_ref, pe_ref, w2_ref, seg_k_ref, kg0_ref)
        kc_ref[0] = kc.astype(BF16)
        vc_ref[0] = vc.astype(BF16)


def _cmp_sample(cache_t, layer, page_table, perm, wr, pe_aug, w2bd, seg_k, kg0, pps):
    bx, n_pages = page_table.shape
    page = cache_t.shape[3]
    past = n_pages * page
    nc = past // CMP_STRIDE
    full = lambda a: pl.BlockSpec(a.shape, lambda b, s, pt: (0,) * a.ndim)
    page_specs = [pl.BlockSpec((1, 1, 2 * KV_W, page),
                               lambda b, s, pt, k=k: (layer, pt[b, s * pps + k], 0, 0))
                  for k in range(pps)]
    return pl.pallas_call(
        functools.partial(_cmp_sample_kernel, pps=pps),
        grid_spec=pltpu.PrefetchScalarGridSpec(
            num_scalar_prefetch=1,
            grid=(bx, n_pages // pps),
            in_specs=page_specs + [full(perm), full(wr), full(pe_aug), full(w2bd), full(seg_k), full(kg0)],
            out_specs=[pl.BlockSpec((1, nc, KV_W), lambda b, s, pt: (b, 0, 0))] * 2,
            scratch_shapes=[pltpu.VMEM((2, nc, CMP_STRIDE * KV_W), BF16)]),
        out_shape=[jax.ShapeDtypeStruct((bx, nc, KV_W), BF16)] * 2,
        compiler_params=_params("parallel", "arbitrary"),
        name="cmp_sample",
    )(page_table, *([cache_t] * pps), perm, wr, pe_aug, w2bd, seg_k, kg0)


def _q_rows(q, lane):
    out = []
    for g in range(N_KV_HEADS):
        keep = (lane // HEAD_DIM) == g
        rows = []
        for r in range(Q_PER_KV):
            h = g * Q_PER_KV + r
            blk = q[:, (h // 2) * LANE:(h // 2 + 1) * LANE]
            if h % 2 != g:
                blk = pltpu.roll(blk, HEAD_DIM, 1)
            rows.append(jnp.where(keep, blk, 0.0))
        out.append(jnp.concatenate(rows, axis=0).astype(BF16))
    return out


def _assemble_heads(vals, lane):
    cols = []
    for k in range(N_Q_HEADS // 2):
        a, b = vals[2 * k], vals[2 * k + 1]
        if (2 * k) // Q_PER_KV == 1:
            a = pltpu.roll(a, HEAD_DIM, 1)
        else:
            b = pltpu.roll(b, HEAD_DIM, 1)
        cols.append(jnp.where(lane < HEAD_DIM, a, b))
    return jnp.concatenate(cols, axis=-1)


def _gate_combine(gates, h, o_cmp, o_slc, o_win):
    c = h * N_BRANCH
    return gates[:, c:c + 1] * o_cmp + gates[:, c + 1:c + 2] * o_slc + gates[:, c + 2:c + 3] * o_win


def _softmax_parts(s3):
    m = jnp.max(s3, axis=-1, keepdims=True)
    e = jnp.exp2(s3 - m)
    return e, jnp.sum(e, axis=-1, keepdims=True)


def _select_rows(imp, cur, n_sel):
    jidx = _iota(imp.shape, 1)
    valid = jidx <= cur
    forced = valid & ((jidx == 0) | (jidx == cur) | (jidx == cur - 1))
    sc = jnp.where(forced, FORCE, jnp.where(valid, imp, NEG))
    rank = jnp.zeros(imp.shape, jnp.int32)
    for i in range(n_sel):
        col = sc[:, i:i + 1]
        beats = (col > sc) | ((col == sc) & (jidx > i))
        rank = rank + jnp.where(beats, 1, 0)
    return (rank < min(N_SLC, n_sel)) & valid


def _select_cols(imp_t, cur, n_sel):
    jidx = _iota(imp_t.shape, 0)
    valid = jidx <= cur
    forced = valid & ((jidx == 0) | (jidx == cur) | (jidx == cur - 1))
    sc = jnp.where(forced, FORCE, jnp.where(valid, imp_t, NEG))
    rank = jnp.zeros(imp_t.shape, jnp.int32)
    for i in range(n_sel):
        row = sc[i:i + 1, :]
        beats = (row > sc) | ((row == sc) & (jidx > i))
        rank = rank + jnp.where(beats, 1, 0)
    return jnp.where((rank < min(N_SLC, n_sel)) & valid, 1.0, 0.0)


def _attn_prompt_kernel(q_ref, kc_ref, vct_ref, ksa_ref, vst_ref, kw_ref, vwt_ref, gates_ref, ovt_ref, o_ref):
    tq = q_ref.shape[1]
    nc = kc_ref.shape[1]
    chunk = vst_ref.shape[3]
    n_sel = ovt_ref.shape[0]
    cols = Q_PER_KV * tq
    groups = range(N_KV_HEADS)
    t0 = pl.program_id(1) * tq
    qpos = t0 + _iota((1, tq), 1)
    per_head = lambda a: jnp.concatenate([a] * Q_PER_KV, axis=1)
    q_t = q_ref[0].T
    zero = jnp.zeros((HEAD_DIM, tq), F32)
    qt = []
    for g in groups:
        blocks = []
        for r in range(Q_PER_KV):
            h = g * Q_PER_KV + r
            head = q_t[h * HEAD_DIM:(h + 1) * HEAD_DIM]
            blocks.append(jnp.concatenate([head, zero] if g == 0 else [zero, head], axis=0))
        qt.append(jnp.concatenate(blocks, axis=1).astype(BF16))
    row_lo = _iota((LANE, cols), 0) < HEAD_DIM
    row_half = _iota((LANE, 1), 0) // HEAD_DIM
    with_ones = lambda vt, g: jnp.where(row_half == g, vt, 1.0)

    def pair(a, normalised):
        o = jnp.where(row_lo, a[0], a[1])
        if normalised:
            return o
        return o / jnp.concatenate([a[0][HEAD_DIM:], a[1][:HEAD_DIM]], axis=0)

    cbias = per_head(jnp.where(_iota((nc, tq), 0) * CMP_STRIDE + CMP_BLOCK - 1 <= qpos, 0.0, NEG))
    cvalid = per_head(jnp.where(qpos >= CMP_BLOCK - 1, 1.0, 0.0))
    cur = qpos // SLC_BLOCK
    o_cmp, rhs = [], []
    for g in groups:
        s = _dot(kc_ref[0], qt[g]) + cbias
        e = jnp.exp2(s - jnp.max(s, axis=0, keepdims=True))
        p = e * (cvalid / jnp.sum(e, axis=0, keepdims=True))
        o_cmp.append(_dot(vct_ref[0], p.astype(BF16)))
        psum = p[:, 0:tq]
        for r in range(1, Q_PER_KV):
            psum = psum + p[:, r * tq:(r + 1) * tq]
        hi, lo = _split_hilo(psum)
        sel = _select_cols(_dot(ovt_ref[...], hi) + _dot(ovt_ref[...], lo), cur, n_sel)
        sel = jnp.concatenate([sel, jnp.zeros((LANE - n_sel, tq), F32)], axis=0)
        selneg = per_head(jnp.where(sel > 0.5, 0.0, NEG)).astype(BF16)
        rhs.append(jnp.concatenate([qt[g], selneg], axis=0))

    def slc_step(c, carry, causal):
        ka = ksa_ref[0, pl.ds(pl.multiple_of(c * chunk, chunk), chunk), :]
        vt = vst_ref[0, c]
        out = []
        for g in groups:
            m, acc = carry[g]
            s = _dot(ka, rhs[g])
            if causal is not None:
                s = jnp.where(causal, s, NEG)
            m_new = jnp.maximum(m, jnp.max(s, axis=0, keepdims=True))
            w = jnp.exp2(s - m_new).astype(BF16)
            out.append((m_new, jnp.exp2(m - m_new) * acc + _dot(with_ones(vt, g), w)))
        return tuple(out)

    init = tuple((jnp.full((1, cols), NEG, F32), jnp.zeros((LANE, cols), F32)) for g in groups)
    c_diag = t0 // chunk
    slc = lax.fori_loop(0, c_diag, lambda c, carry: slc_step(c, carry, None), init)
    if tq == chunk:
        half = tq // 2
        ka = ksa_ref[0, pl.ds(pl.multiple_of(c_diag * chunk, chunk), chunk), :]
        vt = vst_ref[0, c_diag]
        diag = []
        for g in groups:
            m, acc = slc[g]
            accs = []
            for hq in range(2):
                nk = (hq + 1) * half
                pick = lambda a: jnp.concatenate(
                    [a[:, r * tq + hq * half:r * tq + (hq + 1) * half] for r in range(Q_PER_KV)], axis=1)
                causal = per_head(_iota((nk, half), 0) <= hq * half + _iota((1, half), 1))
                s = jnp.where(causal, _dot(ka[0:nk], pick(rhs[g])), NEG)
                m_h = pick(m)
                m_new = jnp.maximum(m_h, jnp.max(s, axis=0, keepdims=True))
                w = jnp.exp2(s - m_new).astype(BF16)
                accs.append(jnp.exp2(m_h - m_new) * pick(acc) + _dot(with_ones(vt[:, 0:nk], g), w))
            diag.append((None, jnp.concatenate([accs[hq][:, r * half:(r + 1) * half] for r in range(Q_PER_KV)
                                                for hq in range(2)], axis=1)))
        slc = diag
    else:
        slc = slc_step(c_diag, slc, per_head(c_diag * chunk + _iota((chunk, tq), 0) <= qpos))

    sub = min(WIN_SUB, tq)
    wn = WINDOW + sub
    parts = [[] for g in groups]
    for hq in range(tq // sub):
        ts = t0 + hq * sub
        wtile0 = jnp.maximum(ts - WINDOW, 0) // LANE
        wstart = pl.multiple_of(wtile0 * LANE, LANE)
        wdiff = (ts + _iota((1, sub), 1)) - (wstart + _iota((wn, sub), 0))
        wbias = per_head(jnp.where((wdiff >= 0) & (wdiff < WINDOW), 0.0, NEG))
        kwin = kw_ref[0, pl.ds(wstart, wn), :]
        vwt = jnp.concatenate([vwt_ref[0, wtile0 + j] for j in range(wn // LANE)], axis=1)
        for g in groups:
            qh = jnp.concatenate([qt[g][:, r * tq + hq * sub:r * tq + (hq + 1) * sub] for r in range(Q_PER_KV)],
                                 axis=1)
            s = _dot(kwin, qh) + wbias
            w = jnp.exp2(s - jnp.max(s, axis=0, keepdims=True)).astype(BF16)
            parts[g].append(_dot(with_ones(vwt, g), w))
    win = [jnp.concatenate([parts[g][hq][:, r * sub:(r + 1) * sub] for r in range(Q_PER_KV)
                            for hq in range(tq // sub)], axis=1) for g in groups]

    branches = (pair(o_cmp, True), pair([slc[g][1] for g in groups], False), pair(win, False))
    gt = gates_ref[0].T
    out = None
    for br in range(N_BRANCH):
        gate = jnp.concatenate(
            [jnp.concatenate([jnp.broadcast_to(gt[(g * Q_PER_KV + r) * N_BRANCH + br][None], (HEAD_DIM, tq))
                              for g in groups], axis=0) for r in range(Q_PER_KV)], axis=1)
        out = gate * branches[br] if out is None else out + gate * branches[br]
    heads = [out[g * HEAD_DIM:(g + 1) * HEAD_DIM, r * tq:(r + 1) * tq] for g in groups for r in range(Q_PER_KV)]
    o_ref[0] = jnp.concatenate(heads, axis=0).T.astype(BF16)


def _attn_prompt(q, kc, vct, ksa, vst, kw, vwt, gates, ovt):
    bx, t, _ = q.shape
    whole = lambda a: pl.BlockSpec((1,) + a.shape[1:], lambda b, i: (b,) + (0,) * (a.ndim - 1))
    const = lambda a: pl.BlockSpec(a.shape, lambda b, i: (0,) * a.ndim)
    return pl.pallas_call(
        _attn_prompt_kernel,
        grid=(bx, t // TQ),
        in_specs=[pl.BlockSpec((1, TQ, ATTN_W), lambda b, i: (b, i, 0)),
                  whole(kc), whole(vct), whole(ksa), whole(vst), whole(kw), whole(vwt),
                  pl.BlockSpec((1, TQ, GATE_PAD), lambda b, i: (b, i, 0)),
                  const(ovt)],
        out_specs=pl.BlockSpec((1, TQ, ATTN_W), lambda b, i: (b, i, 0)),
        out_shape=jax.ShapeDtypeStruct((bx, t, ATTN_W), BF16),
        compiler_params=_params("parallel", "parallel"),
        name="attn_prompt",
    )(q, kc, vct, ksa, vst, kw, vwt, gates, ovt)


def _attn_sample_a_kernel(q_ref, kc_ref, vc_ref, wst_ref, wnew_ref, ov_ref, e_ref, ocmp_ref, owin_ref, bias_ref,
                          wnext_ref, *, pos0, n_cmp, n_sel):
    t = q_ref.shape[1]
    nc = kc_ref.shape[1]
    n_state = wst_ref.shape[3]
    nr = N_Q_HEADS
    lane = _iota((t, LANE), 1)
    tcol = _iota((t, 1), 0)
    qpos = pos0 + tcol
    qbd = jnp.concatenate(_q_rows(q_ref[0], lane), axis=0)
    cidx = _iota((t, nc), 1)
    cbias = jnp.where((cidx < n_cmp) & ((cidx * CMP_STRIDE + CMP_BLOCK - 1) <= qpos), 0.0, NEG)
    cvalid = jnp.where(qpos >= CMP_BLOCK - 1, 1.0, 0.0)
    e, l = _softmax_parts(_dot_nt(qbd, kc_ref[0]).reshape(nr, t, nc) + cbias[None])
    p = e * (cvalid / l)
    ocmp_ref[0] = _dot(p.reshape(nr * t, nc).astype(BF16), vc_ref[0])
    p4 = p.reshape(N_KV_HEADS, Q_PER_KV, t, nc)
    psum = (p4[:, 0] + p4[:, 1] + p4[:, 2] + p4[:, 3]).reshape(N_KV_HEADS * t, nc)
    imp = _dot_hilo(psum, ov_ref[...])
    cur = jnp.concatenate([qpos // SLC_BLOCK] * N_KV_HEADS, axis=0)
    selneg = jnp.where(_select_rows(imp, cur, n_sel), 0.0, NEG).astype(BF16)
    bias_ref[0] = _dot(selneg, e_ref[...]).reshape(N_KV_HEADS, t, e_ref.shape[1])
    wst = wst_ref[0, 0]
    kwt = wst[0:KV_W].astype(BF16)
    vwt = wst[KV_W:2 * KV_W].astype(BF16)
    wnew = _pad_rows(wnew_ref[0], NEW_PAD).astype(BF16)
    sdiff = qpos - (pos0 - n_state + _iota((t, n_state), 1))
    sbias = jnp.where((sdiff >= 0) & (sdiff < WINDOW), 0.0, NEG)
    nbias = jnp.where(tcol - _iota((t, NEW_PAD), 1) >= 0, 0.0, NEG)
    s_a = _dot(qbd, kwt).reshape(nr, t, n_state) + sbias[None]
    s_b = _dot_nt(qbd, wnew[:, 0:KV_W]).reshape(nr, t, NEW_PAD) + nbias[None]
    m = jnp.maximum(jnp.max(s_a, axis=-1, keepdims=True), jnp.max(s_b, axis=-1, keepdims=True))
    e_a = jnp.exp2(s_a - m)
    e_b = jnp.exp2(s_b - m)
    l = jnp.sum(e_a, axis=-1, keepdims=True) + jnp.sum(e_b, axis=-1, keepdims=True)
    o = (_dot_nt(e_a.reshape(nr * t, n_state).astype(BF16), vwt)
         + _dot(e_b.reshape(nr * t, NEW_PAD).astype(BF16), wnew[:, KV_W:2 * KV_W]))
    owin_ref[0] = o / l.reshape(nr * t, 1)
    new_t = _pad_rows(wnew_ref[0], NEW_PAD).T
    tail = jnp.concatenate([jnp.zeros((2 * KV_W, n_state - NEW_PAD), F32), pltpu.roll(new_t, NEW_PAD - t, 1)],
                           axis=1)
    wnext_ref[0] = jnp.where(_iota((1, n_state), 1) < n_state - t, pltpu.roll(wst, n_state - t, 1), tail)


def _attn_sample_a(q, kc, vc, win_state_t, layer, win_new, ov, expand, pos0, n_cmp, n_sel):
    bx, t, _ = q.shape
    nc = kc.shape[1]
    n_state = win_state_t.shape[3]
    kp = expand.shape[1]
    rows = N_Q_HEADS * t
    per_b = lambda a: pl.BlockSpec((1,) + a.shape[1:], lambda b: (b,) + (0,) * (a.ndim - 1))
    const = lambda a: pl.BlockSpec(a.shape, lambda b: (0,) * a.ndim)
    return pl.pallas_call(
        functools.partial(_attn_sample_a_kernel, pos0=pos0, n_cmp=n_cmp, n_sel=n_sel),
        grid=(bx,),
        in_specs=[per_b(q), per_b(kc), per_b(vc),
                  pl.BlockSpec((1, 1, 2 * KV_W, n_state), lambda b: (layer, b, 0, 0)),
                  per_b(win_new), const(ov), const(expand)],
        out_specs=[pl.BlockSpec((1, rows, LANE), lambda b: (b, 0, 0)),
                   pl.BlockSpec((1, rows, LANE), lambda b: (b, 0, 0)),
                   pl.BlockSpec((1, N_KV_HEADS, t, kp), lambda b: (b, 0, 0, 0)),
                   pl.BlockSpec((1, 2 * KV_W, n_state), lambda b: (b, 0, 0))],
        out_shape=[jax.ShapeDtypeStruct((bx, rows, LANE), F32),
                   jax.ShapeDtypeStruct((bx, rows, LANE), F32),
                   jax.ShapeDtypeStruct((bx, N_KV_HEADS, t, kp), F32),
                   jax.ShapeDtypeStruct((bx, 2 * KV_W, n_state), F32)],
        compiler_params=_params("parallel"),
        name="attn_sample_a",
    )(q, kc, vc, win_state_t, win_new, ov, expand)


def _attn_sample_b_kernel(pt_ref, *refs, pps):
    pages = refs[:pps]
    (q_ref, kvn_ref, bias_ref, ocmp_ref, owin_ref, gates_ref, o_ref, s_ref, vt_ref) = refs[pps:]
    t = q_ref.shape[1]
    page = pages[0].shape[3]
    n = pps * page
    nr = N_Q_HEADS
    n_steps = s_ref.shape[0]
    step = pl.program_id(1)
    lane = _iota((t, LANE), 1)
    qbd = jnp.concatenate(_q_rows(q_ref[0], lane), axis=0)
    kt = jnp.concatenate([pages[k][0, 0, 0:KV_W, :].astype(BF16) for k in range(pps)], axis=1)
    s_ref[step] = _dot(qbd, kt)
    for k in range(pps):
        vt_ref[step, :, k * page:(k + 1) * page] = pages[k][0, 0, KV_W:2 * KV_W, :].astype(BF16)

    @pl.when(step == n_steps - 1)
    def _():
        biased = lambda s_flat, bias: (s_flat.reshape(N_KV_HEADS, Q_PER_KV, t, s_flat.shape[-1])
                                       + bias[:, None]).reshape(nr, t, s_flat.shape[-1])
        parts = [biased(s_ref[j], bias_ref[0, :, :, j * n:(j + 1) * n]) for j in range(n_steps)]
        kvn = _pad_rows(kvn_ref[0], NEW_PAD).astype(BF16)
        causal = _iota((t, NEW_PAD), 1) <= _iota((t, 1), 0)
        bias_new = jnp.where(causal[None], bias_ref[0, :, :, n_steps * n:n_steps * n + NEW_PAD], NEG)
        s_new = biased(_dot_nt(qbd, kvn[:, 0:KV_W]), bias_new)
        m = jnp.max(s_new, axis=-1, keepdims=True)
        for p in parts:
            m = jnp.maximum(m, jnp.max(p, axis=-1, keepdims=True))
        e = jnp.exp2(s_new - m)
        l = jnp.sum(e, axis=-1, keepdims=True)
        acc = _dot(e.reshape(nr * t, NEW_PAD).astype(BF16), kvn[:, KV_W:2 * KV_W])
        for j, p in enumerate(parts):
            e = jnp.exp2(p - m)
            l = l + jnp.sum(e, axis=-1, keepdims=True)
            acc = acc + _dot_nt(e.reshape(nr * t, n).astype(BF16), vt_ref[j])
        o_slc = acc.reshape(nr, t, LANE) / l
        gates = gates_ref[0]
        vals = [_gate_combine(gates, h, ocmp_ref[0, h * t:(h + 1) * t], o_slc[h], owin_ref[0, h * t:(h + 1) * t])
                for h in range(nr)]
        o_ref[0] = _assemble_heads(vals, lane)


def _attn_sample_b(cache_t, layer, page_table, q, kv_new, bias, o_cmp, o_win, gates, pps):
    bx, n_pages = page_table.shape
    page = cache_t.shape[3]
    t = q.shape[1]
    rows = N_Q_HEADS * t
    n = pps * page
    page_specs = [pl.BlockSpec((1, 1, 2 * KV_W, page),
                               lambda b, s, pt, k=k: (layer, pt[b, s * pps + k], 1, 0))
                  for k in range(pps)]
    return pl.pallas_call(
        functools.partial(_attn_sample_b_kernel, pps=pps),
        grid_spec=pltpu.PrefetchScalarGridSpec(
            num_scalar_prefetch=1,
            grid=(bx, n_pages // pps),
            in_specs=page_specs + [
                pl.BlockSpec((1, t, ATTN_W), lambda b, s, pt: (b, 0, 0)),
                pl.BlockSpec((1, t, 2 * KV_W), lambda b, s, pt: (b, 0, 1)),
                pl.BlockSpec((1,) + bias.shape[1:], lambda b, s, pt: (b, 0, 0, 0)),
                pl.BlockSpec((1, rows, LANE), lambda b, s, pt: (b, 0, 0)),
                pl.BlockSpec((1, rows, LANE), lambda b, s, pt: (b, 0, 0)),
                pl.BlockSpec((1, t, GATE_PAD), lambda b, s, pt: (b, 0, 0))],
            out_specs=pl.BlockSpec((1, t, ATTN_W), lambda b, s, pt: (b, 0, 0)),
            scratch_shapes=[pltpu.VMEM((n_pages // pps, rows, n), F32),
                            pltpu.VMEM((n_pages // pps, KV_W, n), BF16)]),
        out_shape=jax.ShapeDtypeStruct((bx, t, ATTN_W), F32),
        compiler_params=_params("parallel", "arbitrary"),
        name="attn_sample_b",
    )(page_table, *([cache_t] * pps), q, kv_new, bias, o_cmp, o_win, gates)


def _mix_out_kernel(x_ref, mod_ref, oattn_ref, bg_ref, u_ref, uh_ref, p_ref, ph_ref, cw_ref, cb_ref,
                    pw_ref, ps_ref, wo_ref, norm_ref, x1_ref, h2_ref, *, pos0, zero_first_halo):
    bb, tt, d = x_ref.shape
    rows = bb * tt
    i = pl.program_id(1)
    uh = uh_ref[...]
    ph = ph_ref[...]
    if zero_first_halo:
        keep = jnp.where(i > 0, 1.0, 0.0)
        uh = uh * keep
        ph = ph * keep
    ucat = jnp.concatenate([uh, u_ref[...]], axis=1)
    conv = cb_ref[...]
    for j in range(CONV_K):
        off = CONV_HALO - CONV_BUF + j
        conv = conv + ucat[:, off:off + tt] * cw_ref[j:j + 1]
    y_conv = bg_ref[...] * conv
    pcat = jnp.concatenate([ph, p_ref[...]], axis=1)
    a2 = pcat[:, 1:] + pcat[:, :-1]
    a4 = a2[:, 2:] + a2[:, :-2]
    a8 = a4[:, 4:] + a4[:, :-4]
    a16 = a8[:, 8:] + a8[:, :-8]
    sums = (a2[:, POOL_HALO - 1:POOL_HALO - 1 + tt], a4[:, POOL_HALO - 3:POOL_HALO - 3 + tt],
            a8[:, POOL_HALO - 7:POOL_HALO - 7 + tt], a16[:, POOL_HALO - 15:POOL_HALO - 15 + tt])
    pos = pos0 + i * tt + _iota((1, tt, 1), 1)
    grp = _iota((1, 1, POOL_W), 2) // (POOL_W // len(POOL_WINDOWS))
    mean = jnp.zeros((bb, tt, POOL_W), F32)
    for gi, w in enumerate(POOL_WINDOWS):
        cnt = jnp.minimum(w, pos + 1).astype(F32)
        mean = jnp.where(grp == gi, sums[gi] / cnt, mean)
    dlt = (mean - p_ref[...]).reshape(rows, POOL_W).astype(BF16)
    y_pool = _dot(dlt, pw_ref[...]) * ps_ref[...]
    mix = _dot(oattn_ref[...].reshape(rows, ATTN_W).astype(BF16), wo_ref[0, 0:ATTN_W])
    mix = mix + _dot(y_conv.reshape(rows, CONV_W).astype(BF16), wo_ref[0, ATTN_W:ATTN_W + CONV_W])
    mix = mix + _dot(y_pool.astype(BF16), wo_ref[0, ATTN_W + CONV_W:ATTN_W + CONV_W + POOL_W])
    gate1 = mod_ref[:, :, 2 * d:3 * d]
    x1 = x_ref[...] + gate1 * mix.reshape(bb, tt, d)
    x1_ref[...] = x1
    y = x1 * lax.rsqrt(jnp.mean(x1 * x1, axis=-1, keepdims=True) + EPS) * norm_ref[...]
    h2_ref[...] = (y * (1.0 + mod_ref[:, :, 4 * d:5 * d]) + mod_ref[:, :, 3 * d:4 * d]).astype(h2_ref.dtype)


def _mix_out(x, mod, oattn, bg, u, u_halo, pool, p_halo, conv_w, conv_b, pool_wbd, pool_scale, w_out, layer, norm2,
             bb, tt, pos0, halo_from_self):
    bx, t, d = x.shape
    wo_spec = pl.BlockSpec((1,) + w_out.shape[1:], lambda b, i: (layer, 0, 0), pipeline_mode=pl.Buffered(1))
    blk = lambda w: pl.BlockSpec((bb, tt, w), lambda b, i: (b, i, 0))
    full = lambda a: pl.BlockSpec(a.shape, lambda b, i: (0,) * a.ndim)
    if halo_from_self:
        uh_spec = pl.BlockSpec((bb, CONV_HALO, CONV_W),
                               lambda b, i: (b, jnp.maximum(i * (tt // CONV_HALO) - 1, 0), 0))
        ph_spec = pl.BlockSpec((bb, POOL_HALO, POOL_W),
                               lambda b, i: (b, jnp.maximum(i * (tt // POOL_HALO) - 1, 0), 0))
    else:
        uh_spec = pl.BlockSpec((bb, CONV_HALO, CONV_W), lambda b, i: (b, 0, 0))
        ph_spec = pl.BlockSpec((bb, POOL_HALO, POOL_W), lambda b, i: (b, 0, 0))
    return pl.pallas_call(
        functools.partial(_mix_out_kernel, pos0=pos0, zero_first_halo=halo_from_self),
        grid=(bx // bb, t // tt),
        in_specs=[blk(d), pl.BlockSpec((bb, 1, mod.shape[-1]), lambda b, i: (b, 0, 0)),
                  blk(ATTN_W), blk(CONV_W), blk(CONV_W), uh_spec, blk(POOL_W), ph_spec,
                  full(conv_w), full(conv_b), full(pool_wbd), full(pool_scale), wo_spec, full(norm2)],
        out_specs=[blk(d), blk(d)],
        out_shape=[jax.ShapeDtypeStruct((bx, t, d), F32), jax.ShapeDtypeStruct((bx, t, d), oattn.dtype)],
        compiler_params=_params("parallel", "parallel"),
        name="mix_out",
    )(x, mod, oattn, bg, u, u_halo, pool, p_halo, conv_w, conv_b, pool_wbd, pool_scale, w_out, norm2)


def _ffn_kernel(x1_ref, h2_ref, mod_ref, wa_ref, wb_ref, wd_ref, o_ref):
    bb, tt, d = x1_ref.shape
    h = h2_ref[...].reshape(bb * tt, d).astype(BF16)
    a = _dot(h, wa_ref[0])
    b = _dot(h, wb_ref[0])
    y = _dot((_silu(a) * b).astype(BF16), wd_ref[0])
    o_ref[...] = x1_ref[...] + mod_ref[:, :, 5 * d:6 * d] * y.reshape(bb, tt, d)


def _ffn(x1, h2, mod, w_up, w_down, layer, bb, tt):
    bx, t, d = x1.shape
    hid = w_down.shape[1]
    blk = pl.BlockSpec((bb, tt, d), lambda b, i: (b, i, 0))
    resident = lambda shape, imap: pl.BlockSpec(shape, imap, pipeline_mode=pl.Buffered(1))
    return pl.pallas_call(
        _ffn_kernel,
        grid=(bx // bb, t // tt),
        in_specs=[blk, blk, pl.BlockSpec((bb, 1, mod.shape[-1]), lambda b, i: (b, 0, 0)),
                  resident((1, d, hid), lambda b, i: (layer, 0, 0)),
                  resident((1, d, hid), lambda b, i: (layer, 0, 1)),
                  resident((1, hid, d), lambda b, i: (layer, 0, 0))],
        out_specs=blk,
        out_shape=jax.ShapeDtypeStruct((bx, t, d), F32),
        compiler_params=_params("parallel", "parallel"),
        name="ffn",
    )(x1, h2, mod, w_up, w_up, w_down)


def _overlap_matrix(n_cmp, n_sel, rows, lanes):
    m = np.zeros((rows, lanes), np.float32)
    i = np.arange(n_cmp)
    for part in range(CMP_BLOCK // CMP_STRIDE):
        j = np.minimum((i + part) * CMP_STRIDE // SLC_BLOCK, n_sel - 1)
        np.add.at(m, (i, j), 1.0)
    return m


def _expand_matrix(n_blocks, n_keys):
    return (np.arange(n_blocks)[:, None] == np.arange(n_keys)[None, :] // SLC_BLOCK).astype(np.float32)


def _chunk_order_perm(page):
    pos = np.arange(page)
    m = np.zeros((page, page), np.float32)
    m[(pos % CMP_STRIDE) * (page // CMP_STRIDE) + pos // CMP_STRIDE, pos] = 1.0
    return jnp.asarray(np.stack([m, m.T]), BF16)


def _seg_matrix(n):
    idx = np.arange(n) // HEAD_DIM
    return jnp.asarray((idx[:, None] == idx[None, :]).astype(np.float32) / HEAD_DIM, BF16)


def _prep_weights(w_in, w_out, q_norm, k_norm, cmp_pe, cmp_w1, cmp_w2, pool_w, w_up, w_down):
    depth, d, _ = w_in.shape
    g0 = ATTN_W + 6 * KV_W
    w_in_r = jnp.concatenate([w_in[:, :, :g0], w_in[:, :, g0 + N_Q_HEADS * N_BRANCH:],
                              w_in[:, :, g0:g0 + N_Q_HEADS * N_BRANCH],
                              jnp.zeros((depth, d, GATE_PAD - N_Q_HEADS * N_BRANCH), w_in.dtype)], axis=-1)
    eye_g = jnp.eye(N_KV_HEADS, dtype=BF16)
    w1r = cmp_w1.astype(BF16).reshape(depth, 2, 2, CMP_STRIDE, HEAD_DIM, HEAD_DIM)
    zero = jnp.zeros_like(w1r[:, :, 0])
    wr = jnp.stack([jnp.concatenate([w1r[:, :, 0], zero, w1r[:, :, 1], zero], axis=-1),
                    jnp.concatenate([zero, w1r[:, :, 0], zero, w1r[:, :, 1]], axis=-1)], axis=3)
    wr = wr.reshape(depth, 2, CMP_STRIDE * KV_W, 2 * KV_W)
    w2bd = (cmp_w2.astype(BF16)[:, :, None, :, None, :]
            * eye_g[None, None, :, None, :, None]).reshape(depth, 2, KV_W, KV_W)
    pe = cmp_pe.reshape(depth, 2, 2, CMP_STRIDE, 1, HEAD_DIM)
    pe = jnp.broadcast_to(pe, (depth, 2, 2, CMP_STRIDE, N_KV_HEADS, HEAD_DIM))
    pe = pe.reshape(depth, 2, 2, CMP_STRIDE * KV_W)
    pe_aug = jnp.concatenate([pe, jnp.zeros((depth, 2, 6, CMP_STRIDE * KV_W), F32)], axis=2)
    eye_p = jnp.eye(len(POOL_WINDOWS), dtype=F32)
    pool_bd = (pool_w[:, :, :, None, :] * eye_p[None, :, None, :, None]).reshape(depth, POOL_W, POOL_W)
    return dict(
        w_in=w_in_r.astype(BF16), w_out=w_out.astype(BF16), w_up=w_up.astype(BF16), w_down=w_down.astype(BF16),
        wr=wr.astype(BF16), pe_aug=pe_aug, w2bd=w2bd.astype(BF16), pool_bd=pool_bd.astype(BF16),
        q_gain=jnp.tile(q_norm, (1, N_Q_HEADS))[:, None, :],
        k_gain=jnp.tile(k_norm, (1, 1, N_KV_HEADS)),
    )


def _feature_major_view(a):
    lead = a.shape[:-2]
    f, p = a.shape[-2:]
    a = a.reshape(lead + (f // KV_W, N_KV_HEADS, HEAD_DIM, p))
    n = len(lead)
    return jnp.transpose(a, tuple(range(n)) + (n + 3, n, n + 1, n + 2))


def _layer_prompt(x, mod, l, W, P, C):
    bx, t, d = x.shape
    bb, tt = 1, ROW_TILE
    kg = P["k_gain"][l]
    q, bg, u, pool, gates, kv_t, win_t, raw, ksa, vst, kw, vwt = _proj_in(
        x, mod, W["norm_mix"][l][None, None], P["w_in"], l, P["q_gain"][l], kg[1:3],
        C["seg_q"], C["seg_k"], bb, tt, True)
    kc, vct = _cmp_prompt(raw, P["wr"][l], P["pe_aug"][l], P["w2bd"][l], C["seg_k"], kg[0:1])
    oattn = _attn_prompt(q, kc, vct, ksa, vst, kw, vwt, gates, C["ovt_prompt"])
    x1, h2 = _mix_out(x, mod, oattn, bg, u, u, pool, pool, W["conv_w"][l], W["conv_bias"][l][None],
                      P["pool_bd"][l], W["pool_scale"][l][None], P["w_out"], l, W["norm_ffn"][l][None, None],
                      bb, tt, 0, True)
    x2 = _ffn(x1, h2, mod, P["w_up"], P["w_down"], l, bb, tt)
    states = (kv_t, win_t, u[:, t - CONV_BUF:], pool[:, t - POOL_BUF:])
    return x2, states


def _layer_sample(x, mod, l, W, P, C, cache_t, page_table, win_state_t, conv_state, pool_state):
    bx, t, d = x.shape
    n_pages = page_table.shape[1]
    pos0 = n_pages * cache_t.shape[3]
    pps = C["pps"]
    kg = P["k_gain"][l]
    q, bg, u, pool, gates, kv, win = _proj_in(
        x, mod, W["norm_mix"][l][None, None], P["w_in"], l, P["q_gain"][l], kg[1:3],
        C["seg_q"], C["seg_k"], bx, t, False)
    kc, vc = _cmp_sample(cache_t, l, page_table, C["perm"], P["wr"][l], P["pe_aug"][l], P["w2bd"][l],
                         C["seg_k"], kg[0:1], pps)
    o_cmp, o_win, bias, win_t = _attn_sample_a(q, kc, vc, win_state_t, l, win, C["ov_sample"], C["expand_sample"],
                                               pos0, C["n_cmp_s"], C["n_sel_s"])
    oattn = _attn_sample_b(cache_t, l, page_table, q, kv, bias, o_cmp, o_win, gates, pps)
    u_halo = jnp.concatenate([jnp.zeros((bx, CONV_HALO - CONV_BUF, CONV_W), F32), conv_state], axis=1)
    p_halo = jnp.concatenate([jnp.zeros((bx, POOL_HALO - POOL_BUF, POOL_W), F32), pool_state], axis=1)
    x1, h2 = _mix_out(x, mod, oattn, bg, u, u_halo, pool, p_halo, W["conv_w"][l], W["conv_bias"][l][None],
                      P["pool_bd"][l], W["pool_scale"][l][None], P["w_out"], l, W["norm_ffn"][l][None, None],
                      bx, t, pos0, False)
    x2 = _ffn(x1, h2, mod, P["w_up"], P["w_down"], l, bx, t)
    states =(kv.reshape(bx, t, 4, N_KV_HEADS, HEAD_DIM), win_t,
              jnp.concatenate([conv_state, u], axis=1)[:, t:],
              jnp.concatenate([pool_state, pool], axis=1)[:, t:])
    return x2, states


def kernel(x_prompt, x_sample, cache_nsa_kv, state_win_kv, state_conv, state_pool, page_table, c_prompt, c_sample, norm_mix, norm_ffn, w_ada, b_ada, w_in, w_out, q_norm, k_norm, cmp_pe, cmp_w1, cmp_w2, conv_w, conv_bias, pool_w, pool_scale, w_up, w_down):
    depth = w_in.shape[0]
    bp, tp, d = x_prompt.shape
    bs, ts, _ = x_sample.shape
    n_pages = page_table.shape[1]
    n_pool, page = cache_nsa_kv.shape[1:3]
    past = n_pages * page
    n_state = state_win_kv.shape[2]
    assert tp % ROW_TILE == 0 and tp >= WINDOW + TQ and past % SLC_BLOCK == 0 and ts <= CONV_HALO

    W = dict(norm_mix=norm_mix, norm_ffn=norm_ffn, conv_w=conv_w, conv_bias=conv_bias, pool_scale=pool_scale)
    P = _prep_weights(w_in, w_out, q_norm, k_norm, cmp_pe, cmp_w1, cmp_w2, pool_w, w_up, w_down)
    n_cmp_p = (tp - CMP_BLOCK) // CMP_STRIDE + 1
    n_sel_p = tp // SLC_BLOCK
    n_cmp_s = (past + ts - CMP_BLOCK) // CMP_STRIDE + 1
    n_sel_s = -(-(past + ts) // SLC_BLOCK)
    sel_lanes = -(-n_sel_s // LANE) * LANE
    pps = min(PAGES_PER_STEP, n_pages)
    C = dict(
        seg_q=_seg_matrix(ATTN_W), seg_k=_seg_matrix(KV_W), perm=_chunk_order_perm(page),
        ovt_prompt=jnp.asarray(_overlap_matrix(n_cmp_p, n_sel_p, tp // CMP_STRIDE, n_sel_p).T, BF16),
        ov_sample=jnp.asarray(_overlap_matrix(n_cmp_s, n_sel_s, past // CMP_STRIDE, sel_lanes), BF16),
        expand_sample=jnp.asarray(_expand_matrix(sel_lanes, past + NEW_PAD), BF16),
        n_cmp_s=n_cmp_s, n_sel_s=n_sel_s, pps=pps,
    )
    cache_t = jnp.transpose(cache_nsa_kv, (0, 1, 3, 4, 5, 2)).reshape(depth, n_pool, 4 * KV_W, page)
    win_state_t = jnp.transpose(state_win_kv, (0, 1, 3, 4, 5, 2)).reshape(depth, bs, 2 * KV_W, n_state)
    mod = _ada(jnp.concatenate([c_prompt, c_sample], axis=0), w_ada, b_ada)
    mod_p = mod[:, :bp, None, :]
    mod_s = mod[:, bp:, None, :]

    yp, ys = x_prompt, x_sample
    outs_p, outs_s = [], []
    for l in range(depth):
        yp, st = _layer_prompt(yp, mod_p[l], l, W, P, C)
        outs_p.append(st)
        ys, st = _layer_sample(ys, mod_s[l], l, W, P, C, cache_t, page_table, win_state_t,
                               state_conv[l], state_pool[l])
        outs_s.append(st)
    stack = lambda outs, k: jnp.stack([o[k] for o in outs])
    return (yp, ys, _feature_major_view(stack(outs_p, 0)), stack(outs_s, 0),
            _feature_major_view(stack(outs_p, 1)), _feature_major_view(stack(outs_s, 1)),
            stack(outs_p, 2), stack(outs_s, 2), stack(outs_p, 3), stack(outs_s, 3))
```
